```python
import jax, jax.numpy as jnp
from jax import lax
import numpy as np

D_MODEL = 1024
BATCH = 8
SEQ = 2048
DEPTH = 4

CHUNK = 64
N_MIXERS = 3
PLE_DIM = 256
EPS = 1e-6
MLA_HEADS = 16
MLA_Q_LORA = 384
MLA_KV_LORA = 256
MLA_NOPE = 64
MLA_ROPE = 32
MLA_V = 64
MLA_WIDTH = MLA_HEADS * MLA_V
MLA_IN = MLA_Q_LORA + MLA_KV_LORA + MLA_ROPE + MLA_WIDTH
ROPE_THETA = 10000.0
Q_BLOCK = 128
CONV_WIDTH = 3
CONV_DIM = D_MODEL
CONV_IN = 4 * CONV_DIM
MLSTM_HEADS = 4
MLSTM_INNER = 2 * D_MODEL
MLSTM_DV = MLSTM_INNER // MLSTM_HEADS
MLSTM_DK = MLSTM_DV // 2
MLSTM_IN = 2 * MLSTM_HEADS * MLSTM_DK + 3 * MLSTM_INNER + 2 * MLSTM_HEADS

kernel_name = "hybrid_mla_shortconv_mlstm_ple"


def rmsnorm(x, g=None):
    xf = x.astype(jnp.float32)
    y = xf * lax.rsqrt(jnp.mean(xf * xf, axis=-1, keepdims=True) + EPS)
    if g is not None:
        y = y * g.astype(jnp.float32)
    return y.astype(x.dtype)


def rope_cos_sin(positions, dim, dtype):
    inv = ROPE_THETA ** (-jnp.arange(0, dim, 2, dtype=jnp.float32) / dim)
    ang = positions.astype(jnp.float32)[..., None] * inv
    return jnp.cos(ang).astype(dtype), jnp.sin(ang).astype(dtype)


def apply_rope(x, cos, sin):
    x1, x2 = jnp.split(x, 2, axis=-1)
    return jnp.concatenate([x1 * cos - x2 * sin, x1 * sin + x2 * cos], axis=-1)


def mla_mixer(h, cos, sin, w_in, q_norm, w_q_b, kv_norm, w_kv_b, w_out):
    B, S, _ = h.shape
    u = h @ w_in
    c_q, c_kv, k_rope, z = jnp.split(
        u, [MLA_Q_LORA, MLA_Q_LORA + MLA_KV_LORA, MLA_Q_LORA + MLA_KV_LORA + MLA_ROPE], axis=-1)
    q = (rmsnorm(c_q, q_norm) @ w_q_b).reshape(B, S, MLA_HEADS, MLA_NOPE + MLA_ROPE)
    q_nope = q[..., :MLA_NOPE]
    q_rope = apply_rope(q[..., MLA_NOPE:], cos[:, :, None], sin[:, :, None])
    kv = (rmsnorm(c_kv, kv_norm) @ w_kv_b).reshape(B, S, MLA_HEADS, MLA_NOPE + MLA_V)
    k_nope, v = kv[..., :MLA_NOPE], kv[..., MLA_NOPE:]
    k_rope = apply_rope(k_rope, cos, sin)
    scale = (MLA_NOPE + MLA_ROPE) ** -0.5
    outs = []
    for blk in range(S // Q_BLOCK):
        q0, q1 = blk * Q_BLOCK, (blk + 1) * Q_BLOCK
        s = (jnp.einsum('bqhd,bkhd->bhqk', q_nope[:, q0:q1], k_nope[:, :q1])
             + jnp.einsum('bqhd,bkd->bhqk', q_rope[:, q0:q1], k_rope[:, :q1])).astype(jnp.float32) * scale
        q_chunk = (q0 + jnp.arange(Q_BLOCK)) // CHUNK
        k_chunk = jnp.arange(q1) // CHUNK
        s = jnp.where(k_chunk[None, :] <= q_chunk[:, None], s, -jnp.inf)
        prob = jax.nn.softmax(s, axis=-1).astype(v.dtype)
        outs.append(jnp.einsum('bhqk,bkhd->bqhd', prob, v[:, :q1]))
    o = jnp.concatenate(outs, axis=1).reshape(B, S, MLA_WIDTH)
    return (o * jax.nn.silu(z)) @ w_out


def shortconv_mixer(h, w_in, conv_w, w_out):
    b_gate, c_gate, hx, z = jnp.split(h @ w_in, 4, axis=-1)
    y = lax.conv_general_dilated(
        c_gate * hx, conv_w[:, None, :].astype(hx.dtype), window_strides=(1,),
        padding=[(CONV_WIDTH - 1, 0)], dimension_numbers=('NWC', 'WIO', 'NWC'),
        feature_group_count=CONV_DIM)
    return (b_gate * y * jax.nn.silu(z)) @ w_out


def mlstm_chunk_step(carry, xs):
    C, n, m = carry
    q, k, v, ig, lf = xs
    L = q.shape[2]
    b = jnp.cumsum(lf, axis=-1)
    causal = jnp.tril(jnp.ones((L, L), dtype=bool))
    d_log = jnp.where(causal, b[..., :, None] - b[..., None, :] + ig[..., None, :], -jnp.inf)
    g = b + m[..., None]
    m_t = jnp.maximum(g, jnp.max(d_log, axis=-1))
    w_intra = jnp.exp(d_log - m_t[..., None])
    w_inter = jnp.exp(g - m_t)
    s = jnp.einsum('bhtd,bhsd->bhts', q, k) * w_intra
    num = (w_inter[..., None] * jnp.einsum('bhtd,bhdv->bhtv', q, C)
           + jnp.einsum('bhts,bhsv->bhtv', s, v))
    den = w_inter * jnp.einsum('bhtd,bhd->bht', q, n) + jnp.sum(s, axis=-1)
    h = num / jnp.maximum(jnp.abs(den), jnp.exp(-m_t))[..., None]
    b_last = b[..., -1]
    d_state = b_last[..., None] - b + ig
    m_new = jnp.maximum(b_last + m, jnp.max(d_state, axis=-1))
    w_s = jnp.exp(d_state - m_new[..., None])
    decay = jnp.exp(b_last + m - m_new)
    C = decay[..., None, None] * C + jnp.einsum('bhs,bhsd,bhsv->bhdv', w_s, k, v)
    n = decay[..., None] * n + jnp.einsum('bhs,bhsd->bhd', w_s, k)
    return (C, n, m_new), h


def mlstm_mixer(h, w_in, b_gates, w_out):
    B, S, _ = h.shape
    H, DK, DV = MLSTM_HEADS, MLSTM_DK, MLSTM_DV
    qk = H * DK
    u = h @ w_in
    q, k, v, o, z, gates = jnp.split(
        u, [qk, 2 * qk, 2 * qk + MLSTM_INNER, 2 * qk + 2 * MLSTM_INNER, 2 * qk + 3 * MLSTM_INNER], axis=-1)
    gates = gates.astype(jnp.float32) + b_gates.astype(jnp.float32)
    ig = gates[..., :H]
    lf = jax.nn.log_sigmoid(gates[..., H:])
    nc = S // CHUNK

    def to_chunks(t, d):
        return t.astype(jnp.float32).reshape(B, nc, CHUNK, H, d).transpose(1, 0, 3, 2, 4)

    def gate_chunks(t):
        return t.reshape(B, nc, CHUNK, H).transpose(1, 0, 3, 2)

    xs = (to_chunks(q, DK), to_chunks(k, DK) * (DK ** -0.5), to_chunks(v, DV),
          gate_chunks(ig), gate_chunks(lf))
    init = (jnp.zeros((B, H, DK, DV), jnp.float32), jnp.zeros((B, H, DK), jnp.float32),
            jnp.zeros((B, H), jnp.float32))
    _, hc = lax.scan(mlstm_chunk_step, init, xs)
    hs = hc.transpose(1, 0, 3, 2, 4).reshape(B, S, MLSTM_INNER).astype(h.dtype)
    return (jax.nn.sigmoid(o) * hs * jax.nn.silu(z)) @ w_out


def setup_inputs(seed: int = 0) -> dict:
    key = jax.random.key(seed)
    ks = iter(jax.random.split(key, 32))
    n_mla = max(0, (DEPTH - 0 + 2) // 3)
    n_conv = max(0, (DEPTH - 1 + 2) // 3)
    n_mlstm = max(0, (DEPTH - 2 + 2) // 3)
    res_scale = (2.0 * DEPTH) ** -0.5

    def nrm(shape, scale):
        return jax.random.normal(next(ks), shape, jnp.float32) * scale

    def gain(shape):
        return 1.0 + nrm(shape, 0.02)

    x = nrm((BATCH, SEQ, D_MODEL), 1.0)
    p = nrm((DEPTH, BATCH, SEQ, PLE_DIM), 1.0)
    offsets = jax.random.randint(next(ks), (BATCH, 1), 0, 4096, dtype=jnp.int32)
    positions = (offsets + jnp.arange(SEQ, dtype=jnp.int32)[None, :]).astype(jnp.int32)
    norm_g = gain((DEPTH, D_MODEL))
    mla_w_in = nrm((n_mla, D_MODEL, MLA_IN), D_MODEL ** -0.5)
    mla_q_norm = gain((n_mla, MLA_Q_LORA))
    mla_w_q_b = nrm((n_mla, MLA_Q_LORA, MLA_HEADS * (MLA_NOPE + MLA_ROPE)), MLA_Q_LORA ** -0.5)
    mla_kv_norm = gain((n_mla, MLA_KV_LORA))
    mla_w_kv_b = nrm((n_mla, MLA_KV_LORA, MLA_HEADS * (MLA_NOPE + MLA_V)), MLA_KV_LORA ** -0.5)
    mla_w_out = nrm((n_mla, MLA_WIDTH, D_MODEL), MLA_WIDTH ** -0.5 * res_scale)
    conv_w_in = nrm((n_conv, D_MODEL, CONV_IN), D_MODEL ** -0.5)
    conv_w = nrm((n_conv, CONV_WIDTH, CONV_DIM), CONV_WIDTH ** -0.5)
    conv_w_out = nrm((n_conv, CONV_DIM, D_MODEL), CONV_DIM ** -0.5 * res_scale)
    mlstm_w_in = nrm((n_mlstm, D_MODEL, MLSTM_IN), D_MODEL ** -0.5)
    b_in = nrm((n_mlstm, MLSTM_HEADS), 0.1)
    b_f = jnp.linspace(3.0, 6.0, MLSTM_HEADS, dtype=jnp.float32)[None, :] + nrm((n_mlstm, MLSTM_HEADS), 0.1)
    mlstm_b_gates = jnp.concatenate([b_in, b_f], axis=-1)
    mlstm_w_out = nrm((n_mlstm, MLSTM_INNER, D_MODEL), MLSTM_INNER ** -0.5 * res_scale)
    ple_proj = nrm((DEPTH, PLE_DIM, D_MODEL), PLE_DIM ** -0.5 * res_scale)
    ple_gate = nrm((DEPTH, D_MODEL, D_MODEL), D_MODEL ** -0.5)
    final_norm = gain((D_MODEL,))
    return {"x": x, "p": p, "positions": positions, "norm_g": norm_g,
            "mla_w_in": mla_w_in, "mla_q_norm": mla_q_norm, "mla_w_q_b": mla_w_q_b,
            "mla_kv_norm": mla_kv_norm, "mla_w_kv_b": mla_w_kv_b, "mla_w_out": mla_w_out,
            "conv_w_in": conv_w_in, "conv_w": conv_w, "conv_w_out": conv_w_out,
            "mlstm_w_in": mlstm_w_in, "mlstm_b_gates": mlstm_b_gates, "mlstm_w_out": mlstm_w_out,
            "ple_proj": ple_proj, "ple_gate": ple_gate, "final_norm": final_norm}


def reference(x, p, positions, norm_g, mla_w_in, mla_q_norm, mla_w_q_b, mla_kv_norm, mla_w_kv_b,
              mla_w_out, conv_w_in, conv_w, conv_w_out, mlstm_w_in, mlstm_b_gates, mlstm_w_out,
              ple_proj, ple_gate, final_norm):
    cos, sin = rope_cos_sin(positions, MLA_ROPE, x.dtype)
    for i in range(DEPTH):
        kind, j = i % N_MIXERS, i // N_MIXERS
        h = rmsnorm(x, norm_g[i])
        if kind == 0:
            y = mla_mixer(h, cos, sin, mla_w_in[j], mla_q_norm[j], mla_w_q_b[j],
                          mla_kv_norm[j], mla_w_kv_b[j], mla_w_out[j])
        elif kind == 1:
            y = shortconv_mixer(h, conv_w_in[j], conv_w[j], conv_w_out[j])
        else:
            y = mlstm_mixer(h, mlstm_w_in[j], mlstm_b_gates[j], mlstm_w_out[j])
        x = x + y
        gate = jax.nn.sigmoid(rmsnorm(x) @ ple_gate[i])
        x = x + gate * (p[i] @ ple_proj[i])
    return rmsnorm(x, final_norm)
```

```python
import functools

import jax
import jax.numpy as jnp
from jax import lax
from jax.experimental import pallas as pl
from jax.experimental.pallas import tpu as pltpu

F32 = jnp.float32
BF16 = jnp.bfloat16

D_MODEL = 1024
DEPTH = 4
CHUNK = 64
PLE_DIM = 256
EPS = 1e-6
MLA_HEADS = 16
MLA_Q_LORA = 384
MLA_KV_LORA = 256
MLA_NOPE = 64
MLA_ROPE = 32
MLA_V = 64
ROPE_THETA = 10000.0
CONV_DIM = D_MODEL
MLSTM_HEADS = 4
MLSTM_INNER = 2 * D_MODEL
MLSTM_DV = MLSTM_INNER // MLSTM_HEADS
MLSTM_DK = MLSTM_DV // 2

LANES = 128
MXU_N = 256
HEAD_PAD = 128
NEG = -1e30
VMEM_LIMIT = 48 * 1024 * 1024

TM_OUT = 512
TM_MLA = 256
TQ = 256
TM_CONV = 256
TM_MLSTM = 512
TN_MLSTM = 1024
L_MLSTM = 256


def _rms(x):
    return x * lax.rsqrt(jnp.mean(x * x, axis=-1, keepdims=True) + EPS)


def _dot(a, b):
    return jnp.dot(a, b, preferred_element_type=F32)


def _dot_nt(a, b):
    return lax.dot_general(a, b, (((1,), (1,)), ((), ())), preferred_element_type=F32)


def _silu(z):
    return z * jax.nn.sigmoid(z)


def _params(*sem):
    return pltpu.CompilerParams(dimension_semantics=sem, vmem_limit_bytes=VMEM_LIMIT)


def _out_ple_body(a_ref, w_ref, x_ref, p_ref, pp_ref, pg_ref, fn_ref, o_ref, *, final):
    x1 = x_ref[...] + _dot(a_ref[...], w_ref[...])
    gate = jax.nn.sigmoid(_dot(_rms(x1).astype(BF16), pg_ref[...]))
    x2 = x1 + gate * _dot(p_ref[...].astype(BF16), pp_ref[...])
    if final:
        x2 = _rms(x2) * fn_ref[...]
    o_ref[...] = x2


def _out_ple(a, w_out, x, p_all, layer, ple_proj, ple_gate, final_norm, final):
    t, k = a.shape
    tm = TM_OUT
    const = lambda i: (0, 0)
    return pl.pallas_call(
        functools.partial(_out_ple_body, final=final),
        grid=(t // tm,),
        in_specs=[
            pl.BlockSpec((tm, k), lambda i: (i, 0)),
            pl.BlockSpec((k, D_MODEL), const),
            pl.BlockSpec((tm, D_MODEL), lambda i: (i, 0)),
            pl.BlockSpec((None, tm, PLE_DIM), lambda i: (layer, i, 0)),
            pl.BlockSpec((PLE_DIM, D_MODEL), const),
            pl.BlockSpec((D_MODEL, D_MODEL), const),
            pl.BlockSpec((1, D_MODEL), const),
        ],
        out_specs=pl.BlockSpec((tm, D_MODEL), lambda i: (i, 0)),
        out_shape=jax.ShapeDtypeStruct((t, D_MODEL), F32),
        compiler_params=_params("parallel"),
        name="out_ple",
    )(a, w_out, x, p_all, ple_proj, ple_gate, final_norm)


def _mla_proj_body(x_ref, g_ref, win_ref, qn_ref, wq_ref, kvn_ref, wk_ref, wv_ref, e_ref,
                   tq_ref, tk_ref, q_ref, k_ref, v_ref, zs_ref):
    h = (_rms(x_ref[...]) * g_ref[...]).astype(BF16)
    ckv = _dot(h, win_ref[:, 0:MLA_KV_LORA])
    u2 = _dot(h, win_ref[:, MLA_KV_LORA:MLA_KV_LORA + 512])
    cq = u2[:, 0:MLA_Q_LORA]
    kr = u2[:, MLA_Q_LORA:MLA_Q_LORA + LANES]
    cqn = (_rms(cq) * qn_ref[...]).astype(BF16)
    ckvn = (_rms(ckv) * kvn_ref[...]).astype(BF16)
    tq = tq_ref[...]
    tq2 = jnp.concatenate([tq, tq], axis=1)
    g4 = (kr * tk_ref[...]).astype(BF16)
    for j in range(MLA_HEADS * HEAD_PAD // MXU_N):
        sl = slice(j * MXU_N, (j + 1) * MXU_N)
        q_ref[:, sl] = (_dot(cqn, wq_ref[:, sl]) * tq2).astype(BF16)
        k_ref[:, sl] = (_dot(ckvn, wk_ref[:, sl]) + _dot(g4, e_ref[:, sl])).astype(BF16)
    z0 = MLA_KV_LORA + 512
    for j in range(MLA_HEADS * MLA_V // MXU_N):
        sl = slice(j * MXU_N, (j + 1) * MXU_N)
        v_ref[:, sl] = _dot(ckvn, wv_ref[:, sl]).astype(BF16)
        z = _dot(h, win_ref[:, z0 + j * MXU_N:z0 + (j + 1) * MXU_N])
        zs_ref[:, sl] = _silu(z).astype(BF16)


def _mla_proj(x, g, win, qn, wq, kvn, wk, wv, e, tq_tab, tk_tab):
    t = x.shape[0]
    tm = TM_MLA
    hw = MLA_HEADS * HEAD_PAD
    vw = MLA_HEADS * MLA_V
    const = lambda i: (0, 0)
    row = lambda i: (i, 0)
    full = lambda arr: pl.BlockSpec(arr.shape, const)
    return pl.pallas_call(
        _mla_proj_body,
        grid=(t // tm,),
        in_specs=[pl.BlockSpec((tm, D_MODEL), row), full(g), full(win), full(qn), full(wq), full(kvn),
                  full(wk), full(wv), full(e),
                  pl.BlockSpec((tm, LANES), row), pl.BlockSpec((tm, LANES), row)],
        out_specs=[pl.BlockSpec((tm, hw), row), pl.BlockSpec((tm, hw), row),
                   pl.BlockSpec((tm, vw), row), pl.BlockSpec((tm, vw), row)],
        out_shape=[jax.ShapeDtypeStruct((t, hw), BF16), jax.ShapeDtypeStruct((t, hw), BF16),
                   jax.ShapeDtypeStruct((t, vw), BF16), jax.ShapeDtypeStruct((t, vw), BF16)],
        compiler_params=_params("parallel"),
        name="mla_proj",
    )(x, g, win, qn, wq, kvn, wk, wv, e, tq_tab, tk_tab)


def _mla_attn_body(q_ref, k_ref, v_ref, zs_ref, a_ref, s_scr):
    i = pl.program_id(2)
    row_c = lax.broadcasted_iota(jnp.int32, (TQ, TQ), 0) // CHUNK
    col_c = lax.broadcasted_iota(jnp.int32, (TQ, TQ), 1) // CHUNK
    diag_ok = col_c <= row_c
    outs = []
    for hh in range(2):
        hs = slice(hh * HEAD_PAD, (hh + 1) * HEAD_PAD)
        qh = q_ref[:, hs]

        def scores(j, m, qh=qh, hs=hs):
            kj = k_ref[pl.ds(pl.multiple_of(j * TQ, TQ), TQ), hs]
            s = _dot_nt(qh, kj)
            s_scr[j] = s
            return jnp.maximum(m, jnp.maximum(s[:, :LANES], s[:, LANES:]))

        m = lax.fori_loop(0, i, scores, jnp.full((TQ, LANES), NEG, F32))
        kd = k_ref[pl.ds(pl.multiple_of(i * TQ, TQ), TQ), hs]
        sd = jnp.where(diag_ok, _dot_nt(qh, kd), NEG)
        s_scr[i] = sd
        m = jnp.maximum(m, jnp.maximum(sd[:, :LANES], sd[:, LANES:]))
        mrow = jnp.max(m, axis=1, keepdims=True)

        def weighted(j, carry, mrow=mrow):
            l, acc = carry
            p = jnp.exp(s_scr[j] - mrow)
            l = l + p[:, :LANES] + p[:, LANES:]
            vj = v_ref[pl.ds(pl.multiple_of(j * TQ, TQ), TQ), :]
            return l, acc + _dot(p.astype(BF16), vj)

        zero = jnp.zeros((TQ, LANES), F32)
        l, acc = lax.fori_loop(0, i + 1, weighted, (zero, zero))
        outs.append(acc / jnp.sum(l, axis=1, keepdims=True))
    lane = lax.broadcasted_iota(jnp.int32, (TQ, LANES), 1)
    o = jnp.where(lane < MLA_V, outs[0], outs[1])
    a_ref[...] = (o * zs_ref[...].astype(F32)).astype(BF16)


def _mla_attn(q, k, v, zs, batch, seq):
    t = q.shape[0]
    nq = seq // TQ
    pairs = MLA_HEADS // 2
    return pl.pallas_call(
        _mla_attn_body,
        grid=(batch, pairs, nq),
        in_specs=[
            pl.BlockSpec((TQ, 2 * HEAD_PAD), lambda b, h, i: (b * nq + i, h)),
            pl.BlockSpec((seq, 2 * HEAD_PAD), lambda b, h, i: (b, h)),
            pl.BlockSpec((seq, 2 * MLA_V), lambda b, h, i: (b, h)),
            pl.BlockSpec((TQ, 2 * MLA_V), lambda b, h, i: (b * nq + i, h)),
        ],
        out_specs=pl.BlockSpec((TQ, 2 * MLA_V), lambda b, h, i: (b * nq + i, h)),
        out_shape=jax.ShapeDtypeStruct((t, MLA_HEADS * MLA_V), BF16),
        scratch_shapes=[pltpu.VMEM((nq, TQ, TQ), F32)],
        compiler_params=_params("parallel", "parallel", "arbitrary"),
        name="mla_attn",
    )(q, k, v, zs)


def _conv_body(x_ref, g_ref, win_ref, cw_ref, a_ref, ext_ref):
    tm = TM_CONV
    t = pl.program_id(1)
    h = (_rms(x_ref[...]) * g_ref[...]).astype(BF16)

    @pl.when(t == 0)
    def _():
        ext_ref[0:8, :] = jnp.zeros((8, CONV_DIM), F32)

    @pl.when(t > 0)
    def _():
        ext_ref[0:8, :] = ext_ref[tm:tm + 8, :]

    for c in range(CONV_DIM // MXU_N):
        sl = slice(c * MXU_N, (c + 1) * MXU_N)
        col = lambda part: slice(part * CONV_DIM + c * MXU_N, part * CONV_DIM + (c + 1) * MXU_N)
        prod = _dot(h, win_ref[:, col(1)]) * _dot(h, win_ref[:, col(2)])
        ext_ref[8:8 + tm, sl] = prod
        y = (cw_ref[2:3, sl] * prod + cw_ref[1:2, sl] * ext_ref[7:7 + tm, sl]
             + cw_ref[0:1, sl] * ext_ref[6:6 + tm, sl])
        bg = _dot(h, win_ref[:, col(0)])
        z = _dot(h, win_ref[:, col(3)])
        a_ref[:, sl] = (bg * y * _silu(z)).astype(BF16)


def _conv_mix(x, g, win, cw, batch, seq):
    t = x.shape[0]
    tm = TM_CONV
    nt = seq // tm
    const = lambda b, i: (0, 0)
    row = lambda b, i: (b * nt + i, 0)
    return pl.pallas_call(
        _conv_body,
        grid=(batch, nt),
        in_specs=[pl.BlockSpec((tm, D_MODEL), row), pl.BlockSpec(g.shape, const),
                  pl.BlockSpec(win.shape, const), pl.BlockSpec(cw.shape, const)],
        out_specs=pl.BlockSpec((tm, CONV_DIM), row),
        out_shape=jax.ShapeDtypeStruct((t, CONV_DIM), BF16),
        scratch_shapes=[pltpu.VMEM((tm + 8, CONV_DIM), F32)],
        compiler_params=_params("parallel", "arbitrary"),
        name="conv_mix",
    )(x, g, win, cw)


def _mlstm_proj_body(x_ref, g_ref, w_ref, wg_ref, u_ref, gates_ref, h_scr):
    @pl.when(pl.program_id(1) == 0)
    def _():
        h = (_rms(x_ref[...]) * g_ref[...]).astype(BF16)
        h_scr[...] = h
        gates_ref[...] = _dot(h, wg_ref[...])

    u_ref[...] = _dot(h_scr[...], w_ref[...]).astype(BF16)


def _mlstm_proj(x, g, w, wg):
    t = x.shape[0]
    n = w.shape[1]
    tm, tn = TM_MLSTM, TN_MLSTM
    return pl.pallas_call(
        _mlstm_proj_body,
        grid=(t // tm, n // tn),
        in_specs=[pl.BlockSpec((tm, D_MODEL), lambda i, j: (i, 0)),
                  pl.BlockSpec(g.shape, lambda i, j: (0, 0)),
                  pl.BlockSpec((D_MODEL, tn), lambda i, j: (0, j)),
                  pl.BlockSpec(wg.shape, lambda i, j: (0, 0))],
        out_specs=[pl.BlockSpec((tm, tn), lambda i, j: (i, j)),
                   pl.BlockSpec((tm, LANES), lambda i, j: (i, 0))],
        out_shape=[jax.ShapeDtypeStruct((t, n), BF16), jax.ShapeDtypeStruct((t, LANES), F32)],
        scratch_shapes=[pltpu.VMEM((tm, D_MODEL), BF16)],
        compiler_params=_params("parallel", "arbitrary"),
        name="mlstm_proj",
    )(x, g, w, wg)


def _mlstm_chunk_body(q_ref, k_ref, v_ref, o_ref, z_ref, g_ref, bg_ref, a_ref, c_scr, n_scr, m_scr):
    L = L_MLSTM
    H, DK, DV = MLSTM_HEADS, MLSTM_DK, MLSTM_DV

    @pl.when(pl.program_id(1) == 0)
    def _():
        c_scr[...] = jnp.zeros_like(c_scr)
        n_scr[...] = jnp.zeros_like(n_scr)
        m_scr[...] = jnp.zeros_like(m_scr)

    gates = g_ref[...] + bg_ref[...]
    lf = jnp.minimum(gates, 0.0) - jnp.log(1.0 + jnp.exp(-jnp.abs(gates)))
    r_i = lax.broadcasted_iota(jnp.int32, (L, L), 0)
    c_i = lax.broadcasted_iota(jnp.int32, (L, L), 1)
    causal = c_i <= r_i
    tril = causal.astype(BF16)
    hi = lf.astype(BF16)
    r1 = lf - hi.astype(F32)
    mid = r1.astype(BF16)
    lo = (r1 - mid.astype(F32)).astype(BF16)
    b_all = _dot(tril, hi) + _dot(tril, mid) + _dot(tril, lo)
    b_t = b_all.T
    g_t = gates.T

    for h in range(H):
        bc = b_all[:, H + h:H + h + 1]
        igc = gates[:, h:h + 1]
        br = b_t[H + h:H + h + 1, :]
        igr = g_t[h:h + 1, :]
        m = m_scr[h][0:1, 0:1]
        q = q_ref[:, h * DK:(h + 1) * DK]
        k = k_ref[:, h * DK:(h + 1) * DK] * (DK ** -0.5)
        v = v_ref[:, h * DV:(h + 1) * DV]

        d_log = jnp.where(causal, bc - br + igr, NEG)
        gm = bc + m
        m_t = jnp.maximum(gm, jnp.max(d_log, axis=1, keepdims=True))
        w_intra = jnp.exp(d_log - m_t)
        w_inter = jnp.exp(gm - m_t)
        s = _dot_nt(q, k) * w_intra
        c_old = c_scr[h]
        n_old = n_scr[h][0:1, :]
        num = w_inter * _dot(q, c_old.astype(BF16)) + _dot(s.astype(BF16), v)
        den = (w_inter * jnp.sum(q.astype(F32) * n_old, axis=1, keepdims=True)
               + jnp.sum(s, axis=1, keepdims=True))
        hout = num / jnp.maximum(jnp.abs(den), jnp.exp(-m_t))

        b_last = bc[L - 1:L, :]
        d_state = b_last - bc + igc
        m_new = jnp.maximum(b_last + m, jnp.max(d_state, axis=0, keepdims=True))
        w_s = jnp.exp(d_state - m_new)
        decay = jnp.exp(b_last + m - m_new)
        kw = k.astype(F32) * w_s
        c_scr[h] = decay * c_old + _dot(kw.T.astype(BF16), v)
        n_scr[h] = jnp.broadcast_to(decay * n_old + jnp.sum(kw, axis=0, keepdims=True), (8, DK))
        m_scr[h] = jnp.broadcast_to(m_new, (8, LANES))

        sl = slice(h * DV, (h + 1) * DV)
        og = jax.nn.sigmoid(o_ref[:, sl].astype(F32))
        a_ref[:, sl] = (og * hout * _silu(z_ref[:, sl].astype(F32))).astype(BF16)


def _mlstm_chunk(u, gates, b_gates, batch, seq):
    t = u.shape[0]
    L = L_MLSTM
    nc = seq // L
    qk = MLSTM_HEADS * MLSTM_DK
    row = lambda col: (lambda b, c: (b * nc + c, col))
    return pl.pallas_call(
        _mlstm_chunk_body,
        grid=(batch, nc),
        in_specs=[pl.BlockSpec((L, qk), row(0)), pl.BlockSpec((L, qk), row(1)),
                  pl.BlockSpec((L, MLSTM_INNER), row(1)), pl.BlockSpec((L, MLSTM_INNER), row(2)),
                  pl.BlockSpec((L, MLSTM_INNER), row(3)), pl.BlockSpec((L, LANES), row(0)),
                  pl.BlockSpec((1, LANES), lambda b, c: (0, 0))],
        out_specs=pl.BlockSpec((L, MLSTM_INNER), row(0)),
        out_shape=jax.ShapeDtypeStruct((t, MLSTM_INNER), BF16),
        scratch_shapes=[pltpu.VMEM((MLSTM_HEADS, MLSTM_DK, MLSTM_DV), F32),
                        pltpu.VMEM((MLSTM_HEADS, 8, MLSTM_DK), F32),
                        pltpu.VMEM((MLSTM_HEADS, 8, LANES), F32)],
        compiler_params=_params("parallel", "arbitrary"),
        name="mlstm_chunk",
    )(u, u, u, u, u, gates, b_gates)


def _mla_weights(w_in, w_q_b, w_kv_b):
    half = MLA_ROPE // 2
    c_q = w_in[:, :MLA_Q_LORA]
    c_kv = w_in[:, MLA_Q_LORA:MLA_Q_LORA + MLA_KV_LORA]
    r0 = MLA_Q_LORA + MLA_KV_LORA
    k1 = w_in[:, r0:r0 + half]
    k2 = w_in[:, r0 + half:r0 + MLA_ROPE]
    z = w_in[:, r0 + MLA_ROPE:]
    pad = jnp.zeros((D_MODEL, LANES - 4 * half), w_in.dtype)
    win = jnp.concatenate([c_kv, c_q, k1, k2, k2, k1, pad, z], axis=1).astype(BF16)

    wq = w_q_b.reshape(MLA_Q_LORA, MLA_HEADS, MLA_NOPE + MLA_ROPE)
    r1 = wq[..., MLA_NOPE:MLA_NOPE + half]
    r2 = wq[..., MLA_NOPE + half:]
    wq = jnp.concatenate([wq[..., :MLA_NOPE], r1, r2, r2, r1], axis=-1)
    wq = wq.reshape(MLA_Q_LORA, MLA_HEADS * HEAD_PAD).astype(BF16)

    wkv = w_kv_b.reshape(MLA_KV_LORA, MLA_HEADS, MLA_NOPE + MLA_V)
    wk = jnp.concatenate([wkv[..., :MLA_NOPE], jnp.zeros_like(wkv[..., :HEAD_PAD - MLA_NOPE])], axis=-1)
    wk = wk.reshape(MLA_KV_LORA, MLA_HEADS * HEAD_PAD).astype(BF16)
    wv = wkv[..., MLA_NOPE:].reshape(MLA_KV_LORA, MLA_HEADS * MLA_V).astype(BF16)
    return win, wq, wk, wv


def _rope_placement():
    half = MLA_ROPE // 2
    r = jnp.arange(LANES)
    grp, idx = r // half, r % half
    dest = jnp.where((grp == 0) | (grp == 2), idx, half + idx)
    lane = jnp.arange(HEAD_PAD)
    rot = lane - MLA_NOPE
    hit = (rot[None, :] >= 0) & ((rot[None, :] % MLA_ROPE) == dest[:, None]) & (grp[:, None] < 4)
    e = jnp.tile(hit.astype(BF16), (1, MLA_HEADS))
    return e


def _rope_tables(positions):
    half = MLA_ROPE // 2
    inv = ROPE_THETA ** (-jnp.arange(0, MLA_ROPE, 2, dtype=F32) / MLA_ROPE)
    ang = positions.astype(F32).reshape(-1)[:, None] * inv
    cos, sin = jnp.cos(ang), jnp.sin(ang)
    t = cos.shape[0]
    scale = (MLA_NOPE + MLA_ROPE) ** -0.5
    tq = jnp.concatenate([jnp.ones((t, MLA_NOPE), F32), cos, cos, -sin, sin], axis=1) * scale
    tk = jnp.concatenate([cos, cos, -sin, sin, jnp.zeros((t, LANES - 4 * half), F32)], axis=1)
    return tq, tk


def kernel(x, p, positions, norm_g, mla_w_in, mla_q_norm, mla_w_q_b, mla_kv_norm, mla_w_kv_b, mla_w_out,
           conv_w_in, conv_w, conv_w_out, mlstm_w_in, mlstm_b_gates, mlstm_w_out, ple_proj, ple_gate,
           final_norm):
    batch, seq, d = x.shape
    t = batch * seq
    xs = x.reshape(t, d)
    p_all = p.reshape(DEPTH, t, PLE_DIM)
    tq_tab, tk_tab = _rope_tables(positions)
    e_mat = _rope_placement()
    fn = final_norm.reshape(1, d)
    for i in range(DEPTH):
        kind, j = i % 3, i // 3
        g = norm_g[i].reshape(1, d)
        if kind == 0:
            win, wq, wk, wv = _mla_weights(mla_w_in[j], mla_w_q_b[j], mla_w_kv_b[j])
            q, k, v, zs = _mla_proj(xs, g, win, mla_q_norm[j].reshape(1, -1), wq,
                                    mla_kv_norm[j].reshape(1, -1), wk, wv, e_mat, tq_tab, tk_tab)
            a = _mla_attn(q, k, v, zs, batch, seq)
            w_out = mla_w_out[j]
        elif kind == 1:
            a = _conv_mix(xs, g, conv_w_in[j].astype(BF16), conv_w[j], batch, seq)
            w_out = conv_w_out[j]
        else:
            w = mlstm_w_in[j]
            n_main = w.shape[1] - 2 * MLSTM_HEADS
            wg = jnp.pad(w[:, n_main:], ((0, 0), (0, LANES - 2 * MLSTM_HEADS))).astype(BF16)
            bg = jnp.pad(mlstm_b_gates[j], (0, LANES - 2 * MLSTM_HEADS)).reshape(1, LANES)
            u, gates = _mlstm_proj(xs, g, w[:, :n_main].astype(BF16), wg)
            a = _mlstm_chunk(u, gates, bg, batch, seq)
            w_out = mlstm_w_out[j]
        xs = _out_ple(a, w_out.astype(BF16), xs, p_all, i, ple_proj[i].astype(BF16),
                      ple_gate[i].astype(BF16), fn, final=(i == DEPTH - 1))
    return xs.reshape(batch, seq, d)
```

```python
import functools

import jax
import jax.numpy as jnp
from jax import lax
from jax.experimental import pallas as pl
from jax.experimental.pallas import tpu as pltpu

F32 = jnp.float32
BF16 = jnp.bfloat16

D_MODEL = 1024
DEPTH = 4
CHUNK = 64
PLE_DIM = 256
EPS = 1e-6
MLA_HEADS = 16
MLA_Q_LORA = 384
MLA_KV_LORA = 256
MLA_NOPE = 64
MLA_ROPE = 32
MLA_V = 64
ROPE_THETA = 10000.0
CONV_DIM = D_MODEL
MLSTM_HEADS = 4
MLSTM_INNER = 2 * D_MODEL
MLSTM_DV = MLSTM_INNER // MLSTM_HEADS
MLSTM_DK = MLSTM_DV // 2

LANES = 128
MXU_N = 256
HEAD_PAD = 128
NEG = -1e30
VMEM_LIMIT = 48 * 1024 * 1024

TM_OUT = 512
TM_MLA = 256
TQ = 256
TM_CONV = 256
TM_MLSTM = 512
TN_MLSTM = 1024
L_MLSTM = 256


def _rms(x):
    return x * lax.rsqrt(jnp.mean(x * x, axis=-1, keepdims=True) + EPS)


def _dot(a, b):
    return jnp.dot(a, b, preferred_element_type=F32)


def _dot_nt(a, b):
    return lax.dot_general(a, b, (((1,), (1,)), ((), ())), preferred_element_type=F32)


def _silu(z):
    return z * jax.nn.sigmoid(z)


def _params(*sem):
    return pltpu.CompilerParams(dimension_semantics=sem, vmem_limit_bytes=VMEM_LIMIT)


def _out_ple_body(a_ref, w_ref, x_ref, p_ref, pp_ref, pg_ref, fn_ref, o_ref, *, final):
    x1 = x_ref[...] + _dot(a_ref[...], w_ref[...])
    gate = jax.nn.sigmoid(_dot(_rms(x1).astype(BF16), pg_ref[...]))
    x2 = x1 + gate * _dot(p_ref[...].astype(BF16), pp_ref[...])
    if final:
        x2 = _rms(x2) * fn_ref[...]
    o_ref[...] = x2


def _out_ple(a, w_out, x, p_all, layer, ple_proj, ple_gate, final_norm, final):
    t, k = a.shape
    tm = TM_OUT
    const = lambda i: (0, 0)
    return pl.pallas_call(
        functools.partial(_out_ple_body, final=final),
        grid=(t // tm,),
        in_specs=[
            pl.BlockSpec((tm, k), lambda i: (i, 0)),
            pl.BlockSpec((k, D_MODEL), const),
            pl.BlockSpec((tm, D_MODEL), lambda i: (i, 0)),
            pl.BlockSpec((None, tm, PLE_DIM), lambda i: (layer, i, 0)),
            pl.BlockSpec((PLE_DIM, D_MODEL), const),
            pl.BlockSpec((D_MODEL, D_MODEL), const),
            pl.BlockSpec((1, D_MODEL), const),
        ],
        out_specs=pl.BlockSpec((tm, D_MODEL), lambda i: (i, 0)),
        out_shape=jax.ShapeDtypeStruct((t, D_MODEL), F32),
        compiler_params=_params("parallel"),
        name="out_ple",
    )(a, w_out, x, p_all, ple_proj, ple_gate, final_norm)


def _mla_proj_body(x_ref, g_ref, win_ref, qn_ref, wq_ref, kvn_ref, wk_ref, wv_ref, e_ref,
                   tq_ref, tk_ref, q_ref, k_ref, v_ref, zs_ref):
    h = (_rms(x_ref[...]) * g_ref[...]).astype(BF16)
    ckv = _dot(h, win_ref[:, 0:MLA_KV_LORA])
    u2 = _dot(h, win_ref[:, MLA_KV_LORA:MLA_KV_LORA + 512])
    cq = u2[:, 0:MLA_Q_LORA]
    kr = u2[:, MLA_Q_LORA:MLA_Q_LORA + LANES]
    cqn = (_rms(cq) * qn_ref[...]).astype(BF16)
    ckvn = (_rms(ckv) * kvn_ref[...]).astype(BF16)
    tq = tq_ref[...]
    tq2 = jnp.concatenate([tq, tq], axis=1)
    g4 = (kr * tk_ref[...]).astype(BF16)
    for j in range(MLA_HEADS * HEAD_PAD // MXU_N):
        sl = slice(j * MXU_N, (j + 1) * MXU_N)
        q_ref[:, sl] = (_dot(cqn, wq_ref[:, sl]) * tq2).astype(BF16)
        k_ref[:, sl] = (_dot(ckvn, wk_ref[:, sl]) + _dot(g4, e_ref[:, sl])).astype(BF16)
    z0 = MLA_KV_LORA + 512
    for j in range(MLA_HEADS * MLA_V // MXU_N):
        sl = slice(j * MXU_N, (j + 1) * MXU_N)
        v_ref[:, sl] = _dot(ckvn, wv_ref[:, sl]).astype(BF16)
        z = _dot(h, win_ref[:, z0 + j * MXU_N:z0 + (j + 1) * MXU_N])
        zs_ref[:, sl] = _silu(z).astype(BF16)


def _mla_proj(x, g, win, qn, wq, kvn, wk, wv, e, tq_tab, tk_tab):
    t = x.shape[0]
    tm = TM_MLA
    hw = MLA_HEADS * HEAD_PAD
    vw = MLA_HEADS * MLA_V
    const = lambda i: (0, 0)
    row = lambda i: (i, 0)
    full = lambda arr: pl.BlockSpec(arr.shape, const)
    return pl.pallas_call(
        _mla_proj_body,
        grid=(t // tm,),
        in_specs=[pl.BlockSpec((tm, D_MODEL), row), full(g), full(win), full(qn), full(wq), full(kvn),
                  full(wk), full(wv), full(e),
                  pl.BlockSpec((tm, LANES), row), pl.BlockSpec((tm, LANES), row)],
        out_specs=[pl.BlockSpec((tm, hw), row), pl.BlockSpec((tm, hw), row),
                   pl.BlockSpec((tm, vw), row), pl.BlockSpec((tm, vw), row)],
        out_shape=[jax.ShapeDtypeStruct((t, hw), BF16), jax.ShapeDtypeStruct((t, hw), BF16),
                   jax.ShapeDtypeStruct((t, vw), BF16), jax.ShapeDtypeStruct((t, vw), BF16)],
        compiler_params=_params("parallel"),
        name="mla_proj",
    )(x, g, win, qn, wq, kvn, wk, wv, e, tq_tab, tk_tab)


def _mla_attn_body(q_ref, k_ref, v_ref, zs_ref, a_ref):
    seq = q_ref.shape[0]
    row_c = lax.broadcasted_iota(jnp.int32, (TQ, TQ), 0) // CHUNK
    col_c = lax.broadcasted_iota(jnp.int32, (TQ, TQ), 1) // CHUNK
    diag_ok = col_c <= row_c
    lane = lax.broadcasted_iota(jnp.int32, (TQ, LANES), 1)
    for i in range(seq // TQ):
        rows = slice(i * TQ, (i + 1) * TQ)
        ext = (i + 1) * TQ
        outs = []
        for hh in range(2):
            hs = slice(hh * HEAD_PAD, (hh + 1) * HEAD_PAD)
            qh = q_ref[rows, hs]
            sd = jnp.where(diag_ok, _dot_nt(qh, k_ref[i * TQ:ext, hs]), NEG)
            m = jnp.max(sd, axis=1, keepdims=True)
            if i > 0:
                s0 = _dot_nt(qh, k_ref[0:i * TQ, hs])
                m = jnp.maximum(m, jnp.max(s0, axis=1, keepdims=True))
                p = jnp.concatenate([jnp.exp(s0 - m), jnp.exp(sd - m)], axis=1)
            else:
                p = jnp.exp(sd - m)
            l = jnp.sum(p, axis=1, keepdims=True)
            outs.append(_dot(p.astype(BF16), v_ref[0:ext, :]) / l)
        o = jnp.where(lane < MLA_V, outs[0], outs[1])
        a_ref[rows, :] = (o * zs_ref[rows, :].astype(F32)).astype(BF16)


def _mla_attn(q, k, v, zs, batch, seq):
    t = q.shape[0]
    pairs = MLA_HEADS // 2
    blk = lambda w: pl.BlockSpec((seq, w), lambda b, h: (b, h))
    return pl.pallas_call(
        _mla_attn_body,
        grid=(batch, pairs),
        in_specs=[blk(2 * HEAD_PAD), blk(2 * HEAD_PAD), blk(2 * MLA_V), blk(2 * MLA_V)],
        out_specs=blk(2 * MLA_V),
        out_shape=jax.ShapeDtypeStruct((t, MLA_HEADS * MLA_V), BF16),
        compiler_params=_params("parallel", "parallel"),
        name="mla_attn",
    )(q, k, v, zs)


def _conv_body(x_ref, g_ref, win_ref, cw_ref, a_ref, ext_ref):
    tm = TM_CONV
    t = pl.program_id(1)
    h = (_rms(x_ref[...]) * g_ref[...]).astype(BF16)

    @pl.when(t == 0)
    def _():
        ext_ref[0:8, :] = jnp.zeros((8, CONV_DIM), F32)

    @pl.when(t > 0)
    def _():
        ext_ref[0:8, :] = ext_ref[tm:tm + 8, :]

    for c in range(CONV_DIM // MXU_N):
        sl = slice(c * MXU_N, (c + 1) * MXU_N)
        col = lambda part: slice(part * CONV_DIM + c * MXU_N, part * CONV_DIM + (c + 1) * MXU_N)
        prod = _dot(h, win_ref[:, col(1)]) * _dot(h, win_ref[:, col(2)])
        ext_ref[8:8 + tm, sl] = prod
        y = (cw_ref[2:3, sl] * prod + cw_ref[1:2, sl] * ext_ref[7:7 + tm, sl]
             + cw_ref[0:1, sl] * ext_ref[6:6 + tm, sl])
        bg = _dot(h, win_ref[:, col(0)])
        z = _dot(h, win_ref[:, col(3)])
        a_ref[:, sl] = (bg * y * _silu(z)).astype(BF16)


def _conv_mix(x, g, win, cw, batch, seq):
    t = x.shape[0]
    tm = TM_CONV
    nt = seq // tm
    const = lambda b, i: (0, 0)
    row = lambda b, i: (b * nt + i, 0)
    return pl.pallas_call(
        _conv_body,
        grid=(batch, nt),
        in_specs=[pl.BlockSpec((tm, D_MODEL), row), pl.BlockSpec(g.shape, const),
                  pl.BlockSpec(win.shape, const), pl.BlockSpec(cw.shape, const)],
        out_specs=pl.BlockSpec((tm, CONV_DIM), row),
        out_shape=jax.ShapeDtypeStruct((t, CONV_DIM), BF16),
        scratch_shapes=[pltpu.VMEM((tm + 8, CONV_DIM), F32)],
        compiler_params=_params("parallel", "arbitrary"),
        name="conv_mix",
    )(x, g, win, cw)


def _mlstm_proj_body(x_ref, g_ref, w_ref, wg_ref, u_ref, gates_ref, h_scr):
    h_scr[...] = (_rms(x_ref[...]) * g_ref[...]).astype(BF16)
    gates_ref[...] = _dot(h_scr[...], wg_ref[...])
    for j in range(w_ref.shape[1] // TN_MLSTM):
        sl = slice(j * TN_MLSTM, (j + 1) * TN_MLSTM)
        u_ref[:, sl] = _dot(h_scr[...], w_ref[:, sl]).astype(BF16)


def _mlstm_proj(x, g, w, wg):
    t = x.shape[0]
    n = w.shape[1]
    tm = TM_MLSTM
    const = lambda i: (0, 0)
    row = lambda i: (i, 0)
    return pl.pallas_call(
        _mlstm_proj_body,
        grid=(t // tm,),
        in_specs=[pl.BlockSpec((tm, D_MODEL), row),
                  pl.BlockSpec(g.shape, const),
                  pl.BlockSpec(w.shape, const, pipeline_mode=pl.Buffered(1)),
                  pl.BlockSpec(wg.shape, const)],
        out_specs=[pl.BlockSpec((tm, n), row), pl.BlockSpec((tm, LANES), row)],
        out_shape=[jax.ShapeDtypeStruct((t, n), BF16), jax.ShapeDtypeStruct((t, LANES), F32)],
        scratch_shapes=[pltpu.VMEM((tm, D_MODEL), BF16)],
        compiler_params=_params("parallel"),
        name="mlstm_proj",
    )(x, g, w, wg)


def _mlstm_chunk_body(q_ref, k_ref, v_ref, o_ref, z_ref, g_ref, bg_ref, a_ref, c_scr, n_scr, m_scr):
    L = L_MLSTM
    H, DK, DV = MLSTM_HEADS, MLSTM_DK, MLSTM_DV

    @pl.when(pl.program_id(1) == 0)
    def _():
        c_scr[...] = jnp.zeros_like(c_scr)
        n_scr[...] = jnp.zeros_like(n_scr)
        m_scr[...] = jnp.zeros_like(m_scr)

    gates = g_ref[...] + bg_ref[...]
    lf = jnp.minimum(gates, 0.0) - jnp.log(1.0 + jnp.exp(-jnp.abs(gates)))
    r_i = lax.broadcasted_iota(jnp.int32, (L, L), 0)
    c_i = lax.broadcasted_iota(jnp.int32, (L, L), 1)
    causal = c_i <= r_i
    tril = causal.astype(BF16)
    hi = lf.astype(BF16)
    r1 = lf - hi.astype(F32)
    mid = r1.astype(BF16)
    lo = (r1 - mid.astype(F32)).astype(BF16)
    b_all = _dot(tril, hi) + _dot(tril, mid) + _dot(tril, lo)
    b_t = b_all.T
    g_t = gates.T

    for h in range(H):
        bc = b_all[:, H + h:H + h + 1]
        igc = gates[:, h:h + 1]
        br = b_t[H + h:H + h + 1, :]
        igr = g_t[h:h + 1, :]
        m = m_scr[h][0:1, 0:1]
        q = q_ref[:, h * DK:(h + 1) * DK]
        k = k_ref[:, h * DK:(h + 1) * DK] * (DK ** -0.5)
        v = v_ref[:, h * DV:(h + 1) * DV]

        d_log = jnp.where(causal, bc - br + igr, NEG)
        gm = bc + m
        m_t = jnp.maximum(gm, jnp.max(d_log, axis=1, keepdims=True))
        w_intra = jnp.exp(d_log - m_t)
        w_inter = jnp.exp(gm - m_t)
        s = _dot_nt(q, k) * w_intra
        c_old = c_scr[h]
        n_old = n_scr[h][0:1, :]
        num = w_inter * _dot(q, c_old.astype(BF16)) + _dot(s.astype(BF16), v)
        den = (w_inter * jnp.sum(q.astype(F32) * n_old, axis=1, keepdims=True)
               + jnp.sum(s, axis=1, keepdims=True))
        hout = num / jnp.maximum(jnp.abs(den), jnp.exp(-m_t))

        b_last = bc[L - 1:L, :]
        d_state = b_last - bc + igc
        m_new = jnp.maximum(b_last + m, jnp.max(d_state, axis=0, keepdims=True))
        w_s = jnp.exp(d_state - m_new)
        decay = jnp.exp(b_last + m - m_new)
        kw = k.astype(F32) * w_s
        c_scr[h] = decay * c_old + _dot(kw.T.astype(BF16), v)
        n_scr[h] = jnp.broadcast_to(decay * n_old + jnp.sum(kw, axis=0, keepdims=True), (8, DK))
        m_scr[h] = jnp.broadcast_to(m_new, (8, LANES))

        sl = slice(h * DV, (h + 1) * DV)
        og = jax.nn.sigmoid(o_ref[:, sl].astype(F32))
        a_ref[:, sl] = (og * hout * _silu(z_ref[:, sl].astype(F32))).astype(BF16)


def _mlstm_chunk(u, gates, b_gates, batch, seq):
    t = u.shape[0]
    L = L_MLSTM
    nc = seq // L
    qk = MLSTM_HEADS * MLSTM_DK
    row = lambda col: (lambda b, c: (b * nc + c, col))
    return pl.pallas_call(
        _mlstm_chunk_body,
        grid=(batch, nc),
        in_specs=[pl.BlockSpec((L, qk), row(0)), pl.BlockSpec((L, qk), row(1)),
                  pl.BlockSpec((L, MLSTM_INNER), row(1)), pl.BlockSpec((L, MLSTM_INNER), row(2)),
                  pl.BlockSpec((L, MLSTM_INNER), row(3)), pl.BlockSpec((L, LANES), row(0)),
                  pl.BlockSpec((1, LANES), lambda b, c: (0, 0))],
        out_specs=pl.BlockSpec((L, MLSTM_INNER), row(0)),
        out_shape=jax.ShapeDtypeStruct((t, MLSTM_INNER), BF16),
        scratch_shapes=[pltpu.VMEM((MLSTM_HEADS, MLSTM_DK, MLSTM_DV), F32),
                        pltpu.VMEM((MLSTM_HEADS, 8, MLSTM_DK), F32),
                        pltpu.VMEM((MLSTM_HEADS, 8, LANES), F32)],
        compiler_params=_params("parallel", "arbitrary"),
        name="mlstm_chunk",
    )(u, u, u, u, u, gates, b_gates)


def _mla_weights(w_in, w_q_b, w_kv_b):
    half = MLA_ROPE // 2
    c_q = w_in[:, :MLA_Q_LORA]
    c_kv = w_in[:, MLA_Q_LORA:MLA_Q_LORA + MLA_KV_LORA]
    r0 = MLA_Q_LORA + MLA_KV_LORA
    k1 = w_in[:, r0:r0 + half]
    k2 = w_in[:, r0 + half:r0 + MLA_ROPE]
    z = w_in[:, r0 + MLA_ROPE:]
    pad = jnp.zeros((D_MODEL, LANES - 4 * half), w_in.dtype)
    win = jnp.concatenate([c_kv, c_q, k1, k2, k2, k1, pad, z], axis=1).astype(BF16)

    wq = w_q_b.reshape(MLA_Q_LORA, MLA_HEADS, MLA_NOPE + MLA_ROPE)
    r1 = wq[..., MLA_NOPE:MLA_NOPE + half]
    r2 = wq[..., MLA_NOPE + half:]
    wq = jnp.concatenate([wq[..., :MLA_NOPE], r1, r2, r2, r1], axis=-1)
    wq = wq.reshape(MLA_Q_LORA, MLA_HEADS * HEAD_PAD).astype(BF16)

    wkv = w_kv_b.reshape(MLA_KV_LORA, MLA_HEADS, MLA_NOPE + MLA_V)
    wk = jnp.concatenate([wkv[..., :MLA_NOPE], jnp.zeros_like(wkv[..., :HEAD_PAD - MLA_NOPE])], axis=-1)
    wk = wk.reshape(MLA_KV_LORA, MLA_HEADS * HEAD_PAD).astype(BF16)
    wv = wkv[..., MLA_NOPE:].reshape(MLA_KV_LORA, MLA_HEADS * MLA_V).astype(BF16)
    return win, wq, wk, wv


def _rope_placement():
    half = MLA_ROPE // 2
    r = jnp.arange(LANES)
    grp, idx = r // half, r % half
    dest = jnp.where((grp == 0) | (grp == 2), idx, half + idx)
    lane = jnp.arange(HEAD_PAD)
    rot = lane - MLA_NOPE
    hit = (rot[None, :] >= 0) & ((rot[None, :] % MLA_ROPE) == dest[:, None]) & (grp[:, None] < 4)
    e = jnp.tile(hit.astype(BF16), (1, MLA_HEADS))
    return e


def _rope_tables(positions):
    half = MLA_ROPE // 2
    inv = ROPE_THETA ** (-jnp.arange(0, MLA_ROPE, 2, dtype=F32) / MLA_ROPE)
    ang = positions.astype(F32).reshape(-1)[:, None] * inv
    cos, sin = jnp.cos(ang), jnp.sin(ang)
    t = cos.shape[0]
    scale = (MLA_NOPE + MLA_ROPE) ** -0.5
    tq = jnp.concatenate([jnp.ones((t, MLA_NOPE), F32), cos, cos, -sin, sin], axis=1) * scale
    tk = jnp.concatenate([cos, cos, -sin, sin, jnp.zeros((t, LANES - 4 * half), F32)], axis=1)
    return tq, tk


def kernel(x, p, positions, norm_g, mla_w_in, mla_q_norm, mla_w_q_b, mla_kv_norm, mla_w_kv_b, mla_w_out,
           conv_w_in, conv_w, conv_w_out, mlstm_w_in, mlstm_b_gates, mlstm_w_out, ple_proj, ple_gate,
           final_norm):
    batch, seq, d = x.shape
    t = batch * seq
    xs = x.reshape(t, d)
    p_all = p.reshape(DEPTH, t, PLE_DIM)
    tq_tab, tk_tab = _rope_tables(positions)
    e_mat = _rope_placement()
    fn = final_norm.reshape(1, d)
    for i in range(DEPTH):
        kind, j = i % 3, i // 3
        g = norm_g[i].reshape(1, d)
        if kind == 0:
            win, wq, wk, wv = _mla_weights(mla_w_in[j], mla_w_q_b[j], mla_w_kv_b[j])
            q, k, v, zs = _mla_proj(xs, g, win, mla_q_norm[j].reshape(1, -1), wq,
                                    mla_kv_norm[j].reshape(1, -1), wk, wv, e_mat, tq_tab, tk_tab)
            a = _mla_attn(q, k, v, zs, batch, seq)
            w_out = mla_w_out[j]
        elif kind == 1:
            a = _conv_mix(xs, g, conv_w_in[j].astype(BF16), conv_w[j], batch, seq)
            w_out = conv_w_out[j]
        else:
            w = mlstm_w_in[j]
            n_main = w.shape[1] - 2 * MLSTM_HEADS
            wg = jnp.pad(w[:, n_main:], ((0, 0), (0, LANES - 2 * MLSTM_HEADS))).astype(BF16)
            bg = jnp.pad(mlstm_b_gates[j], (0, LANES - 2 * MLSTM_HEADS)).reshape(1, LANES)
            u, gates = _mlstm_proj(xs, g, w[:, :n_main].astype(BF16), wg)
            a = _mlstm_chunk(u, gates, bg, batch, seq)
            w_out = mlstm_w_out[j]
        xs = _out_ple(a, w_out.astype(BF16), xs, p_all, i, ple_proj[i].astype(BF16),
                      ple_gate[i].astype(BF16), fn, final=(i == DEPTH - 1))
    return xs.reshape(batch, seq, d)
```

```python
import functools

import jax
import jax.numpy as jnp
from jax import lax
from jax.experimental import pallas as pl
from jax.experimental.pallas import tpu as pltpu

F32 = jnp.float32
BF16 = jnp.bfloat16

D_MODEL = 1024
DEPTH = 4
CHUNK = 64
PLE_DIM = 256
EPS = 1e-6
MLA_HEADS = 16
MLA_Q_LORA = 384
MLA_KV_LORA = 256
MLA_NOPE = 64
MLA_ROPE = 32
MLA_V = 64
ROPE_THETA = 10000.0
CONV_DIM = D_MODEL
MLSTM_HEADS = 4
MLSTM_INNER = 2 * D_MODEL
MLSTM_DV = MLSTM_INNER // MLSTM_HEADS
MLSTM_DK = MLSTM_DV // 2

LANES = 128
MXU_N = 256
HEAD_PAD = 128
NEG = -1e30
LOG2E = 1.4426950408889634
VMEM_LIMIT = 48 * 1024 * 1024

TM_OUT = 512
TM_MLA = 256
TQ = 256
TM_CONV = 256
TM_MLSTM = 512
TN_MLSTM = 1024
L_MLSTM = 256


def _rms(x):
    return x * lax.rsqrt(jnp.mean(x * x, axis=-1, keepdims=True) + EPS)


def _dot(a, b):
    return jnp.dot(a, b, preferred_element_type=F32)


def _dot_nt(a, b):
    return lax.dot_general(a, b, (((1,), (1,)), ((), ())), preferred_element_type=F32)


def _silu(z):
    return z * jax.nn.sigmoid(z)


def _params(*sem, flags=None):
    return pltpu.CompilerParams(dimension_semantics=sem, vmem_limit_bytes=VMEM_LIMIT, flags=flags)


def _out_ple_body(a_ref, w_ref, x_ref, p_ref, pp_ref, pg_ref, fn_ref, o_ref, *, final):
    x1 = x_ref[...] + _dot(a_ref[...], w_ref[...])
    gate = jax.nn.sigmoid(_dot(_rms(x1).astype(BF16), pg_ref[...]))
    x2 = x1 + gate * _dot(p_ref[...].astype(BF16), pp_ref[...])
    if final:
        x2 = _rms(x2) * fn_ref[...]
    o_ref[...] = x2


def _out_ple(a, w_out, x, p_all, layer, ple_proj, ple_gate, final_norm, final):
    t, k = a.shape
    tm = TM_OUT
    const = lambda i: (0, 0)
    return pl.pallas_call(
        functools.partial(_out_ple_body, final=final),
        grid=(t // tm,),
        in_specs=[
            pl.BlockSpec((tm, k), lambda i: (i, 0)),
            pl.BlockSpec((k, D_MODEL), const),
            pl.BlockSpec((tm, D_MODEL), lambda i: (i, 0)),
            pl.BlockSpec((None, tm, PLE_DIM), lambda i: (layer, i, 0)),
            pl.BlockSpec((PLE_DIM, D_MODEL), const),
            pl.BlockSpec((D_MODEL, D_MODEL), const),
            pl.BlockSpec((1, D_MODEL), const),
        ],
        out_specs=pl.BlockSpec((tm, D_MODEL), lambda i: (i, 0)),
        out_shape=jax.ShapeDtypeStruct((t, D_MODEL), F32),
        compiler_params=_params("parallel"),
        name="out_ple",
    )(a, w_out, x, p_all, ple_proj, ple_gate, final_norm)


def _mla_proj_body(x_ref, g_ref, win_ref, qn_ref, wq_ref, kvn_ref, wk_ref, wvt_ref, e_ref,
                   tq_ref, tk_ref, q_ref, k_ref, vt_ref, zs_ref):
    h = (_rms(x_ref[...]) * g_ref[...]).astype(BF16)
    ckv = _dot(h, win_ref[:, 0:MLA_KV_LORA])
    u2 = _dot(h, win_ref[:, MLA_KV_LORA:MLA_KV_LORA + 512])
    cq = u2[:, 0:MLA_Q_LORA]
    kr = u2[:, MLA_Q_LORA:MLA_Q_LORA + LANES]
    cqn = (_rms(cq) * qn_ref[...]).astype(BF16)
    ckvn = (_rms(ckv) * kvn_ref[...]).astype(BF16)
    tq = tq_ref[...]
    tq2 = jnp.concatenate([tq, tq], axis=1)
    g4 = (kr * tk_ref[...]).astype(BF16)
    for j in range(MLA_HEADS * HEAD_PAD // MXU_N):
        sl = slice(j * MXU_N, (j + 1) * MXU_N)
        q_ref[:, sl] = (_dot(cqn, wq_ref[:, sl]) * tq2).astype(BF16)
        k_ref[:, sl] = (_dot(ckvn, wk_ref[:, sl]) + _dot(g4, e_ref[:, sl])).astype(BF16)
    z0 = MLA_KV_LORA + 512
    for j in range(MLA_HEADS * MLA_V // MXU_N):
        sl = slice(j * MXU_N, (j + 1) * MXU_N)
        vt_ref[sl, :] = _dot_nt(wvt_ref[sl, :], ckvn).astype(BF16)
        z = _dot(h, win_ref[:, z0 + j * MXU_N:z0 + (j + 1) * MXU_N])
        zs_ref[:, sl] = _silu(z).astype(BF16)


def _mla_proj(x, g, win, qn, wq, kvn, wk, wvt, e, tq_tab, tk_tab):
    t = x.shape[0]
    tm = TM_MLA
    hw = MLA_HEADS * HEAD_PAD
    vw = MLA_HEADS * MLA_V
    const = lambda i: (0, 0)
    row = lambda i: (i, 0)
    full = lambda arr: pl.BlockSpec(arr.shape, const)
    return pl.pallas_call(
        _mla_proj_body,
        grid=(t // tm,),
        in_specs=[pl.BlockSpec((tm, D_MODEL), row), full(g), full(win), full(qn), full(wq), full(kvn),
                  full(wk), full(wvt), full(e),
                  pl.BlockSpec((tm, LANES), row), pl.BlockSpec((tm, LANES), row)],
        out_specs=[pl.BlockSpec((tm, hw), row), pl.BlockSpec((tm, hw), row),
                   pl.BlockSpec((vw, tm), lambda i: (0, i)), pl.BlockSpec((tm, vw), row)],
        out_shape=[jax.ShapeDtypeStruct((t, hw), BF16), jax.ShapeDtypeStruct((t, hw), BF16),
                   jax.ShapeDtypeStruct((vw, t), BF16), jax.ShapeDtypeStruct((t, vw), BF16)],
        compiler_params=_params("parallel"),
        name="mla_proj",
    )(x, g, win, qn, wq, kvn, wk, wvt, e, tq_tab, tk_tab)


def _mla_attn_body(q_ref, k_ref, vt_ref, zs_ref, a_ref):
    seq = q_ref.shape[0]
    nq = seq // TQ
    key_c = lax.broadcasted_iota(jnp.int32, (TQ, TQ), 0) // CHUNK
    qry_c = lax.broadcasted_iota(jnp.int32, (TQ, TQ), 1) // CHUNK
    diag_ok = key_c <= qry_c
    units = [(i, hh, j) for i in range(nq) for hh in range(2) for j in range(i + 1)]
    lead = nq + 2
    scores, probs = {}, {}
    m_run, l_run, acc = {}, {}, {}
    outs = {}

    def score_unit(u):
        i, hh, j = units[u]
        hs = slice(hh * HEAD_PAD, (hh + 1) * HEAD_PAD)
        s = _dot_nt(k_ref[j * TQ:(j + 1) * TQ, hs], q_ref[i * TQ:(i + 1) * TQ, hs])
        if j == i:
            s = jnp.where(diag_ok, s, NEG)
        mj = jnp.max(s, axis=0, keepdims=True)
        m_run[i, hh] = mj if j == 0 else jnp.maximum(m_run[i, hh], mj)
        scores[u] = s

    def exp_unit(u):
        i, hh, j = units[u]
        p = jnp.exp2(scores.pop(u) - m_run[i, hh])
        lj = jnp.sum(p, axis=0, keepdims=True)
        l_run[i, hh] = lj if j == 0 else l_run[i, hh] + lj
        probs[u] = p.astype(BF16)

    def value_unit(u):
        i, hh, j = units[u]
        oj = _dot(vt_ref[hh * MLA_V:(hh + 1) * MLA_V, j * TQ:(j + 1) * TQ], probs.pop(u))
        acc[i, hh] = oj if j == 0 else acc[i, hh] + oj

    def finish_unit(u):
        i, hh, j = units[u]
        if j != i:
            return
        outs[hh] = acc.pop((i, hh)) * (1.0 / l_run.pop((i, hh)))
        if hh == 1:
            rows = slice(i * TQ, (i + 1) * TQ)
            o = jnp.concatenate([outs.pop(0), outs.pop(1)], axis=0).T
            a_ref[rows, :] = (o * zs_ref[rows, :].astype(F32)).astype(BF16)

    n_units = len(units)
    for t in range(-lead, n_units + 3):
        if 0 <= t + lead < n_units:
            score_unit(t + lead)
        if 0 <= t - 1 < n_units:
            value_unit(t - 1)
        if 0 <= t < n_units:
            exp_unit(t)
        if 0 <= t - 3 < n_units:
            finish_unit(t - 3)


def _mla_attn(q, k, vt, zs, batch, seq):
    t = q.shape[0]
    pairs = MLA_HEADS // 2
    blk = lambda w: pl.BlockSpec((seq, w), lambda b, h: (b, h))
    return pl.pallas_call(
        _mla_attn_body,
        grid=(batch, pairs),
        in_specs=[blk(2 * HEAD_PAD), blk(2 * HEAD_PAD),
                  pl.BlockSpec((2 * MLA_V, seq), lambda b, h: (h, b)), blk(2 * MLA_V)],
        out_specs=blk(2 * MLA_V),
        out_shape=jax.ShapeDtypeStruct((t, MLA_HEADS * MLA_V), BF16),
        compiler_params=_params("parallel", "parallel"),
        name="mla_attn",
    )(q, k, vt, zs)


def _conv_body(x_ref, g_ref, win_ref, cw_ref, a_ref, ext_ref):
    tm = TM_CONV
    t = pl.program_id(1)
    h = (_rms(x_ref[...]) * g_ref[...]).astype(BF16)

    @pl.when(t == 0)
    def _():
        ext_ref[0:8, :] = jnp.zeros((8, CONV_DIM), F32)

    @pl.when(t > 0)
    def _():
        ext_ref[0:8, :] = ext_ref[tm:tm + 8, :]

    for c in range(CONV_DIM // MXU_N):
        sl = slice(c * MXU_N, (c + 1) * MXU_N)
        col = lambda part: slice(part * CONV_DIM + c * MXU_N, part * CONV_DIM + (c + 1) * MXU_N)
        prod = _dot(h, win_ref[:, col(1)]) * _dot(h, win_ref[:, col(2)])
        ext_ref[8:8 + tm, sl] = prod
        y = (cw_ref[2:3, sl] * prod + cw_ref[1:2, sl] * ext_ref[7:7 + tm, sl]
             + cw_ref[0:1, sl] * ext_ref[6:6 + tm, sl])
        bg = _dot(h, win_ref[:, col(0)])
        z = _dot(h, win_ref[:, col(3)])
        a_ref[:, sl] = (bg * y * _silu(z)).astype(BF16)


def _conv_mix(x, g, win, cw, batch, seq):
    t = x.shape[0]
    tm = TM_CONV
    nt = seq // tm
    const = lambda b, i: (0, 0)
    row = lambda b, i: (b * nt + i, 0)
    return pl.pallas_call(
        _conv_body,
        grid=(batch, nt),
        in_specs=[pl.BlockSpec((tm, D_MODEL), row), pl.BlockSpec(g.shape, const),
                  pl.BlockSpec(win.shape, const), pl.BlockSpec(cw.shape, const)],
        out_specs=pl.BlockSpec((tm, CONV_DIM), row),
        out_shape=jax.ShapeDtypeStruct((t, CONV_DIM), BF16),
        scratch_shapes=[pltpu.VMEM((tm + 8, CONV_DIM), F32)],
        compiler_params=_params("parallel", "arbitrary"),
        name="conv_mix",
    )(x, g, win, cw)


def _mlstm_proj_body(x_ref, g_ref, w_ref, wg_ref, u_ref, gates_ref, h_scr):
    h_scr[...] = (_rms(x_ref[...]) * g_ref[...]).astype(BF16)
    gates_ref[...] = _dot(h_scr[...], wg_ref[...])
    for j in range(w_ref.shape[1] // TN_MLSTM):
        sl = slice(j * TN_MLSTM, (j + 1) * TN_MLSTM)
        u_ref[:, sl] = _dot(h_scr[...], w_ref[:, sl]).astype(BF16)


def _mlstm_proj(x, g, w, wg):
    t = x.shape[0]
    n = w.shape[1] - 2 * MLSTM_HEADS
    tm = TM_MLSTM
    const = lambda i: (0, 0)
    row = lambda i: (i, 0)
    return pl.pallas_call(
        _mlstm_proj_body,
        grid=(t // tm,),
        in_specs=[pl.BlockSpec((tm, D_MODEL), row),
                  pl.BlockSpec(g.shape, const),
                  pl.BlockSpec((D_MODEL, n), const, pipeline_mode=pl.Buffered(1)),
                  pl.BlockSpec(wg.shape, const)],
        out_specs=[pl.BlockSpec((tm, n), row), pl.BlockSpec((tm, LANES), row)],
        out_shape=[jax.ShapeDtypeStruct((t, n), BF16), jax.ShapeDtypeStruct((t, LANES), F32)],
        scratch_shapes=[pltpu.VMEM((tm, D_MODEL), BF16)],
        compiler_params=_params("parallel"),
        name="mlstm_proj",
    )(x, g, w, wg)


def _mlstm_chunk_body(q_ref, k_ref, v_ref, o_ref, z_ref, g_ref, bg_ref, a_ref, c_scr, n_scr, m_scr):
    L = L_MLSTM
    H, DK, DV = MLSTM_HEADS, MLSTM_DK, MLSTM_DV

    @pl.when(pl.program_id(1) == 0)
    def _():
        c_scr[...] = jnp.zeros_like(c_scr)
        n_scr[...] = jnp.zeros_like(n_scr)
        m_scr[...] = jnp.zeros_like(m_scr)

    gates = g_ref[...] + bg_ref[...]
    lf = jnp.minimum(gates, 0.0) - jnp.log(1.0 + jnp.exp(-jnp.abs(gates)))
    r_i = lax.broadcasted_iota(jnp.int32, (L, L), 0)
    c_i = lax.broadcasted_iota(jnp.int32, (L, L), 1)
    causal = c_i <= r_i
    tril = causal.astype(BF16)
    hi = lf.astype(BF16)
    r1 = lf - hi.astype(F32)
    mid = r1.astype(BF16)
    lo = (r1 - mid.astype(F32)).astype(BF16)
    b_all = _dot(tril, hi) + _dot(tril, mid) + _dot(tril, lo)
    b_t = b_all.T
    g_t = gates.T

    for h in range(H):
        bc = b_all[:, H + h:H + h + 1]
        igc = gates[:, h:h + 1]
        br = b_t[H + h:H + h + 1, :]
        igr = g_t[h:h + 1, :]
        m = m_scr[h][0:1, 0:1]
        q = q_ref[:, h * DK:(h + 1) * DK]
        k = k_ref[:, h * DK:(h + 1) * DK] * (DK ** -0.5)
        v = v_ref[:, h * DV:(h + 1) * DV]

        d_log = jnp.where(causal, bc - br + igr, NEG)
        gm = bc + m
        m_t = jnp.maximum(gm, jnp.max(d_log, axis=1, keepdims=True))
        w_intra = jnp.exp(d_log - m_t)
        w_inter = jnp.exp(gm - m_t)
        s = _dot_nt(q, k) * w_intra
        c_old = c_scr[h]
        n_old = n_scr[h][0:1, :]
        num = w_inter * _dot(q, c_old.astype(BF16)) + _dot(s.astype(BF16), v)
        den = (w_inter * jnp.sum(q.astype(F32) * n_old, axis=1, keepdims=True)
               + jnp.sum(s, axis=1, keepdims=True))
        hout = num / jnp.maximum(jnp.abs(den), jnp.exp(-m_t))

        b_last = bc[L - 1:L, :]
        d_state = b_last - bc + igc
        m_new = jnp.maximum(b_last + m, jnp.max(d_state, axis=0, keepdims=True))
        w_s = jnp.exp(d_state - m_new)
        decay = jnp.exp(b_last + m - m_new)
        kw = k.astype(F32) * w_s
        c_scr[h] = decay * c_old + _dot(kw.T.astype(BF16), v)
        n_scr[h] = jnp.broadcast_to(decay * n_old + jnp.sum(kw, axis=0, keepdims=True), (8, DK))
        m_scr[h] = jnp.broadcast_to(m_new, (8, LANES))

        sl = slice(h * DV, (h + 1) * DV)
        og = jax.nn.sigmoid(o_ref[:, sl].astype(F32))
        a_ref[:, sl] = (og * hout * _silu(z_ref[:, sl].astype(F32))).astype(BF16)


def _mlstm_chunk(u, gates, b_gates, batch, seq):
    t = u.shape[0]
    L = L_MLSTM
    nc = seq // L
    qk = MLSTM_HEADS * MLSTM_DK
    row = lambda col: (lambda b, c: (b * nc + c, col))
    return pl.pallas_call(
        _mlstm_chunk_body,
        grid=(batch, nc),
        in_specs=[pl.BlockSpec((L, qk), row(0)), pl.BlockSpec((L, qk), row(1)),
                  pl.BlockSpec((L, MLSTM_INNER), row(1)), pl.BlockSpec((L, MLSTM_INNER), row(2)),
                  pl.BlockSpec((L, MLSTM_INNER), row(3)), pl.BlockSpec((L, LANES), row(0)),
                  pl.BlockSpec((1, LANES), lambda b, c: (0, 0))],
        out_specs=pl.BlockSpec((L, MLSTM_INNER), row(0)),
        out_shape=jax.ShapeDtypeStruct((t, MLSTM_INNER), BF16),
        scratch_shapes=[pltpu.VMEM((MLSTM_HEADS, MLSTM_DK, MLSTM_DV), F32),
                        pltpu.VMEM((MLSTM_HEADS, 8, MLSTM_DK), F32),
                        pltpu.VMEM((MLSTM_HEADS, 8, LANES), F32)],
        compiler_params=_params("parallel", "arbitrary"),
        name="mlstm_chunk",
    )(u, u, u, u, u, gates, b_gates)


def _mla_weights(w_in, w_q_b, w_kv_b):
    half = MLA_ROPE // 2
    c_q = w_in[:, :MLA_Q_LORA]
    c_kv = w_in[:, MLA_Q_LORA:MLA_Q_LORA + MLA_KV_LORA]
    r0 = MLA_Q_LORA + MLA_KV_LORA
    k1 = w_in[:, r0:r0 + half]
    k2 = w_in[:, r0 + half:r0 + MLA_ROPE]
    z = w_in[:, r0 + MLA_ROPE:]
    pad = jnp.zeros((D_MODEL, LANES - 4 * half), w_in.dtype)
    win = jnp.concatenate([c_kv, c_q, pad, k1, k2, k2, k1, z], axis=1).astype(BF16)

    wq = w_q_b.reshape(MLA_Q_LORA, MLA_HEADS, MLA_NOPE + MLA_ROPE)
    r1 = wq[..., MLA_NOPE:MLA_NOPE + half]
    r2 = wq[..., MLA_NOPE + half:]
    wq = jnp.concatenate([wq[..., :MLA_NOPE], r1, r2, r2, r1], axis=-1)
    wq = wq.reshape(MLA_Q_LORA, MLA_HEADS * HEAD_PAD).astype(BF16)

    wkv = w_kv_b.reshape(MLA_KV_LORA, MLA_HEADS, MLA_NOPE + MLA_V)
    wk = jnp.concatenate([wkv[..., :MLA_NOPE], jnp.zeros_like(wkv[..., :HEAD_PAD - MLA_NOPE])], axis=-1)
    wk = wk.reshape(MLA_KV_LORA, MLA_HEADS * HEAD_PAD).astype(BF16)
    wvt = wkv[..., MLA_NOPE:].reshape(MLA_KV_LORA, MLA_HEADS * MLA_V).T.astype(BF16)
    return win, wq, wk, wvt


def _rope_placement():
    half = MLA_ROPE // 2
    r = jnp.arange(LANES) - MLA_NOPE
    grp, idx = r // half, r % half
    dest = jnp.where((grp == 0) | (grp == 2), idx, half + idx)
    rot = jnp.arange(HEAD_PAD) - MLA_NOPE
    hit = (rot[None, :] >= 0) & ((rot[None, :] % MLA_ROPE) == dest[:, None]) & (r[:, None] >= 0)
    return jnp.tile(hit.astype(BF16), (1, MLA_HEADS))


def _rope_tables(positions):
    half = MLA_ROPE // 2
    inv = ROPE_THETA ** (-jnp.arange(0, MLA_ROPE, 2, dtype=F32) / MLA_ROPE)
    inv_lane = jnp.concatenate([jnp.zeros((MLA_NOPE,), F32), jnp.tile(inv, 4)])
    ang = positions.astype(F32).reshape(-1)[:, None] * inv_lane[None, :]
    grp = ((jnp.arange(LANES) - MLA_NOPE) // half)[None, :]
    sin = jnp.sin(ang)
    rot = jnp.where(grp < 2, jnp.cos(ang), jnp.where(grp == 2, -sin, sin))
    scale = (MLA_NOPE + MLA_ROPE) ** -0.5 * LOG2E
    tq = jnp.where(grp < 0, 1.0, rot) * scale
    tk = jnp.where(grp < 0, 0.0, rot)
    return tq, tk


def kernel(x, p, positions, norm_g, mla_w_in, mla_q_norm, mla_w_q_b, mla_kv_norm, mla_w_kv_b, mla_w_out,
           conv_w_in, conv_w, conv_w_out, mlstm_w_in, mlstm_b_gates, mlstm_w_out, ple_proj, ple_gate,
           final_norm):
    batch, seq, d = x.shape
    t = batch * seq
    xs = x.reshape(t, d)
    p_all = p.reshape(DEPTH, t, PLE_DIM)
    tq_tab, tk_tab = _rope_tables(positions)
    e_mat = _rope_placement()
    fn = final_norm.reshape(1, d)
    for i in range(DEPTH):
        kind, j = i % 3, i // 3
        g = norm_g[i].reshape(1, d)
        if kind == 0:
            win, wq, wk, wvt = _mla_weights(mla_w_in[j], mla_w_q_b[j], mla_w_kv_b[j])
            q, k, vt, zs = _mla_proj(xs, g, win, mla_q_norm[j].reshape(1, -1), wq,
                                     mla_kv_norm[j].reshape(1, -1), wk, wvt, e_mat, tq_tab, tk_tab)
            a = _mla_attn(q, k, vt, zs, batch, seq)
            w_out = mla_w_out[j]
        elif kind == 1:
            a = _conv_mix(xs, g, conv_w_in[j].astype(BF16), conv_w[j], batch, seq)
            w_out = conv_w_out[j]
        else:
            w = mlstm_w_in[j]
            n_main = w.shape[1] - 2 * MLSTM_HEADS
            wg = jnp.pad(w[:, n_main:], ((0, 0), (0, LANES - 2 * MLSTM_HEADS))).astype(BF16)
            bg = jnp.pad(mlstm_b_gates[j], (0, LANES - 2 * MLSTM_HEADS)).reshape(1, LANES)
            u, gates = _mlstm_proj(xs, g, w.astype(BF16), wg)
            a = _mlstm_chunk(u, gates, bg, batch, seq)
            w_out = mlstm_w_out[j]
        xs = _out_ple(a, w_out.astype(BF16), xs, p_all, i, ple_proj[i].astype(BF16),
                      ple_gate[i].astype(BF16), fn, final=(i == DEPTH - 1))
    return xs.reshape(batch, seq, d)
```

```python
import functools

import jax
import jax.numpy as jnp
from jax import lax
from jax.experimental import pallas as pl
from jax.experimental.pallas import tpu as pltpu

F32 = jnp.float32
BF16 = jnp.bfloat16

D_MODEL = 1024
DEPTH = 4
CHUNK = 64
PLE_DIM = 256
EPS = 1e-6
MLA_HEADS = 16
MLA_Q_LORA = 384
MLA_KV_LORA = 256
MLA_NOPE = 64
MLA_ROPE = 32
MLA_V = 64
ROPE_THETA = 10000.0
CONV_DIM = D_MODEL
MLSTM_HEADS = 4
MLSTM_INNER = 2 * D_MODEL
MLSTM_DV = MLSTM_INNER // MLSTM_HEADS
MLSTM_DK = MLSTM_DV // 2
MLSTM_U = 2 * MLSTM_HEADS * MLSTM_DK + 2 * MLSTM_INNER

LANES = 128
MXU_N = 256
HEAD_PAD = 128
NEG = -1e30
LOG2E = 1.4426950408889634
VMEM_LIMIT = 48 * 1024 * 1024

TM_OUT = 512
TM_MLA = 256
TQ = 256
TM_CONV = 256
TM_MLSTM = 512
TN_MLSTM = 1024
L_MLSTM = 256


def _rms(x):
    return x * lax.rsqrt(jnp.mean(x * x, axis=-1, keepdims=True) + EPS)


def _dot(a, b):
    return jnp.dot(a, b, preferred_element_type=F32)


def _dot_nt(a, b):
    return lax.dot_general(a, b, (((1,), (1,)), ((), ())), preferred_element_type=F32)


def _silu(z):
    return z * jax.nn.sigmoid(z)


def _params(*sem, flags=None):
    return pltpu.CompilerParams(dimension_semantics=sem, vmem_limit_bytes=VMEM_LIMIT, flags=flags)


def _out_ple_body(a_ref, w_ref, x_ref, p_ref, pp_ref, pg_ref, fn_ref, o_ref, *, final):
    x1 = x_ref[...] + _dot(a_ref[...], w_ref[...])
    gate = jax.nn.sigmoid(_dot(_rms(x1).astype(BF16), pg_ref[...]))
    x2 = x1 + gate * _dot(p_ref[...].astype(BF16), pp_ref[...])
    if final:
        x2 = _rms(x2) * fn_ref[...]
    o_ref[...] = x2


def _out_ple(a, w_out, x, p_all, layer, ple_proj, ple_gate, final_norm, final):
    t, k = a.shape
    tm = TM_OUT
    const = lambda i: (0, 0)
    return pl.pallas_call(
        functools.partial(_out_ple_body, final=final),
        grid=(t // tm,),
        in_specs=[
            pl.BlockSpec((tm, k), lambda i: (i, 0)),
            pl.BlockSpec((k, D_MODEL), const),
            pl.BlockSpec((tm, D_MODEL), lambda i: (i, 0)),
            pl.BlockSpec((None, tm, PLE_DIM), lambda i: (layer, i, 0)),
            pl.BlockSpec((PLE_DIM, D_MODEL), const),
            pl.BlockSpec((D_MODEL, D_MODEL), const),
            pl.BlockSpec((1, D_MODEL), const),
        ],
        out_specs=pl.BlockSpec((tm, D_MODEL), lambda i: (i, 0)),
        out_shape=jax.ShapeDtypeStruct((t, D_MODEL), F32),
        compiler_params=_params("parallel"),
        name="out_ple",
    )(a, w_out, x, p_all, ple_proj, ple_gate, final_norm)


def _mla_proj_body(x_ref, g_ref, win_ref, qn_ref, wq_ref, kvn_ref, wk_ref, wvt_ref,
                   tq_ref, tk_ref, q_ref, k_ref, vt_ref, zs_ref):
    h = (_rms(x_ref[...]) * g_ref[...]).astype(BF16)
    ckv = _dot(h, win_ref[:, 0:MLA_KV_LORA])
    u2 = _dot(h, win_ref[:, MLA_KV_LORA:MLA_KV_LORA + 512])
    cq = u2[:, 0:MLA_Q_LORA]
    kr = u2[:, MLA_Q_LORA:MLA_Q_LORA + LANES]
    cqn = (_rms(cq) * qn_ref[...]).astype(BF16)
    ckvn = (_rms(ckv) * kvn_ref[...]).astype(BF16)
    lane = lax.broadcasted_iota(jnp.int32, kr.shape, 1)
    low = lane < MLA_NOPE
    quarter = MLA_ROPE
    g4 = kr * tk_ref[...]
    kr_hi = g4 + jnp.where(lane < MLA_NOPE + quarter, pltpu.roll(g4, LANES - quarter, 1),
                           pltpu.roll(g4, quarter, 1))
    kr_lo = pltpu.roll(kr_hi, MLA_NOPE, 1)
    tq = tq_ref[...]
    tq2 = jnp.concatenate([tq, pltpu.roll(tq, MLA_NOPE, 1)], axis=1)
    for j in range(MLA_HEADS * HEAD_PAD // MXU_N):
        sl = slice(j * MXU_N, (j + 1) * MXU_N)
        q_ref[:, sl] = (_dot(cqn, wq_ref[:, sl]) * tq2).astype(BF16)
    for c in range(MLA_HEADS * MLA_NOPE // MXU_N):
        kn = _dot(ckvn, wk_ref[:, c * MXU_N:(c + 1) * MXU_N])
        for half in range(2):
            pair = kn[:, half * LANES:(half + 1) * LANES]
            col = (2 * c + half) * MXU_N
            k_ref[:, col:col + LANES] = jnp.where(low, pair, kr_hi).astype(BF16)
            k_ref[:, col + LANES:col + MXU_N] = jnp.where(low, kr_lo, pair).astype(BF16)
    z0 = MLA_KV_LORA + 512
    for j in range(MLA_HEADS * MLA_V // MXU_N):
        sl = slice(j * MXU_N, (j + 1) * MXU_N)
        vt_ref[sl, :] = _dot_nt(wvt_ref[sl, :], ckvn).astype(BF16)
        z = _dot(h, win_ref[:, z0 + j * MXU_N:z0 + (j + 1) * MXU_N])
        zs_ref[:, sl] = _silu(z).astype(BF16)


def _mla_proj(x, g, win, qn, wq, kvn, wk, wvt, tq_tab, tk_tab):
    t = x.shape[0]
    tm = TM_MLA
    hw = MLA_HEADS * HEAD_PAD
    vw = MLA_HEADS * MLA_V
    const = lambda i: (0, 0)
    row = lambda i: (i, 0)
    full = lambda arr: pl.BlockSpec(arr.shape, const)
    return pl.pallas_call(
        _mla_proj_body,
        grid=(t // tm,),
        in_specs=[pl.BlockSpec((tm, D_MODEL), row), full(g), full(win), full(qn), full(wq), full(kvn),
                  full(wk), full(wvt),
                  pl.BlockSpec((tm, LANES), row), pl.BlockSpec((tm, LANES), row)],
        out_specs=[pl.BlockSpec((tm, hw), row), pl.BlockSpec((tm, hw), row),
                   pl.BlockSpec((vw, tm), lambda i: (0, i)), pl.BlockSpec((tm, vw), row)],
        out_shape=[jax.ShapeDtypeStruct((t, hw), BF16), jax.ShapeDtypeStruct((t, hw), BF16),
                   jax.ShapeDtypeStruct((vw, t), BF16), jax.ShapeDtypeStruct((t, vw), BF16)],
        compiler_params=_params("parallel"),
        name="mla_proj",
    )(x, g, win, qn, wq, kvn, wk, wvt, tq_tab, tk_tab)


def _mla_attn_body(q_ref, k_ref, vt_ref, zs_ref, a_ref):
    seq = q_ref.shape[0]
    nq = seq // TQ
    key_c = lax.broadcasted_iota(jnp.int32, (TQ, TQ), 0) // CHUNK
    qry_c = lax.broadcasted_iota(jnp.int32, (TQ, TQ), 1) // CHUNK
    diag_ok = key_c <= qry_c
    units = [(i, hh, j) for i in range(nq) for hh in range(2) for j in range(i + 1)]
    lead = nq + 2
    lag = 5
    scores, probs = {}, {}
    m_run, l_run, acc = {}, {}, {}
    outs = {}

    def score_unit(u):
        i, hh, j = units[u]
        hs = slice(hh * HEAD_PAD, (hh + 1) * HEAD_PAD)
        s = _dot_nt(k_ref[j * TQ:(j + 1) * TQ, hs], q_ref[i * TQ:(i + 1) * TQ, hs])
        if j == i:
            s = jnp.where(diag_ok, s, NEG)
        mj = jnp.max(s, axis=0, keepdims=True)
        m_run[i, hh] = mj if j == 0 else jnp.maximum(m_run[i, hh], mj)
        scores[u] = s

    def exp_unit(u):
        i, hh, j = units[u]
        p = jnp.exp2(scores.pop(u) - m_run[i, hh])
        lj = jnp.sum(p, axis=0, keepdims=True)
        l_run[i, hh] = lj if j == 0 else l_run[i, hh] + lj
        probs[u] = p.astype(BF16)

    def value_unit(u):
        i, hh, j = units[u]
        oj = _dot(vt_ref[hh * MLA_V:(hh + 1) * MLA_V, j * TQ:(j + 1) * TQ], probs.pop(u))
        acc[i, hh] = oj if j == 0 else acc[i, hh] + oj

    def finish_unit(u):
        i, hh, j = units[u]
        if j != i:
            return
        outs[hh] = acc.pop((i, hh)) * (1.0 / l_run.pop((i, hh)))
        if hh == 1:
            rows = slice(i * TQ, (i + 1) * TQ)
            o = jnp.concatenate([outs.pop(0), outs.pop(1)], axis=0).T
            a_ref[rows, :] = (o * zs_ref[rows, :].astype(F32)).astype(BF16)

    n_units = len(units)
    for t in range(-lead, n_units + lag + 3):
        if 0 <= t + lead < n_units:
            score_unit(t + lead)
        if 0 <= t - lag < n_units:
            value_unit(t - lag)
        if 0 <= t < n_units:
            exp_unit(t)
        if 0 <= t - lag - 2 < n_units:
            finish_unit(t - lag - 2)


def _mla_attn(q, k, vt, zs, batch, seq):
    t = q.shape[0]
    pairs = MLA_HEADS // 2
    blk = lambda w: pl.BlockSpec((seq, w), lambda b, h: (b, h))
    return pl.pallas_call(
        _mla_attn_body,
        grid=(batch, pairs),
        in_specs=[blk(2 * HEAD_PAD), blk(2 * HEAD_PAD),
                  pl.BlockSpec((2 * MLA_V, seq), lambda b, h: (h, b)), blk(2 * MLA_V)],
        out_specs=blk(2 * MLA_V),
        out_shape=jax.ShapeDtypeStruct((t, MLA_HEADS * MLA_V), BF16),
        compiler_params=_params("parallel", "parallel"),
        name="mla_attn",
    )(q, k, vt, zs)


def _conv_body(x_ref, g_ref, win_ref, cw_ref, a_ref, ext_ref):
    tm = TM_CONV
    t = pl.program_id(1)
    h = (_rms(x_ref[...]) * g_ref[...]).astype(BF16)

    @pl.when(t == 0)
    def _():
        ext_ref[0:8, :] = jnp.zeros((8, CONV_DIM), F32)

    @pl.when(t > 0)
    def _():
        ext_ref[0:8, :] = ext_ref[tm:tm + 8, :]

    for c in range(CONV_DIM // MXU_N):
        sl = slice(c * MXU_N, (c + 1) * MXU_N)
        col = lambda part: slice(part * CONV_DIM + c * MXU_N, part * CONV_DIM + (c + 1) * MXU_N)
        prod = _dot(h, win_ref[:, col(1)]) * _dot(h, win_ref[:, col(2)])
        ext_ref[8:8 + tm, sl] = prod
        y = (cw_ref[2:3, sl] * prod + cw_ref[1:2, sl] * ext_ref[7:7 + tm, sl]
             + cw_ref[0:1, sl] * ext_ref[6:6 + tm, sl])
        bg = _dot(h, win_ref[:, col(0)])
        z = _dot(h, win_ref[:, col(3)])
        a_ref[:, sl] = (bg * y * _silu(z)).astype(BF16)


def _conv_mix(x, g, win, cw, batch, seq):
    t = x.shape[0]
    tm = TM_CONV
    nt = seq // tm
    const = lambda b, i: (0, 0)
    row = lambda b, i: (b * nt + i, 0)
    return pl.pallas_call(
        _conv_body,
        grid=(batch, nt),
        in_specs=[pl.BlockSpec((tm, D_MODEL), row), pl.BlockSpec(g.shape, const),
                  pl.BlockSpec(win.shape, const), pl.BlockSpec(cw.shape, const)],
        out_specs=pl.BlockSpec((tm, CONV_DIM), row),
        out_shape=jax.ShapeDtypeStruct((t, CONV_DIM), BF16),
        scratch_shapes=[pltpu.VMEM((tm + 8, CONV_DIM), F32)],
        compiler_params=_params("parallel", "arbitrary"),
        name="conv_mix",
    )(x, g, win, cw)


def _mlstm_proj_body(x_ref, g_ref, w_ref, wg_ref, u_ref, gates_ref, h_scr):
    h_scr[...] = (_rms(x_ref[...]) * g_ref[...]).astype(BF16)
    gates_ref[...] = _dot(h_scr[...], wg_ref[...])
    tn = TN_MLSTM
    qk = MLSTM_HEADS * MLSTM_DK
    cols = lambda start, j: slice(start + j * tn, start + (j + 1) * tn)
    proj = lambda sl: _dot(h_scr[...], w_ref[:, sl])
    for j in range(qk // tn):
        u_ref[:, cols(0, j)] = proj(cols(0, j)).astype(BF16)
        u_ref[:, cols(qk, j)] = (proj(cols(qk, j)) * (MLSTM_DK ** -0.5)).astype(BF16)
    for j in range(MLSTM_INNER // tn):
        u_ref[:, cols(2 * qk, j)] = proj(cols(2 * qk, j)).astype(BF16)
        o = proj(cols(2 * qk + MLSTM_INNER, j))
        z = proj(cols(2 * qk + 2 * MLSTM_INNER, j))
        u_ref[:, cols(2 * qk + MLSTM_INNER, j)] = (jax.nn.sigmoid(o) * _silu(z)).astype(BF16)


def _mlstm_proj(x, g, w, wg):
    t = x.shape[0]
    n = w.shape[1] - 2 * MLSTM_HEADS
    tm = TM_MLSTM
    const = lambda i: (0, 0)
    row = lambda i: (i, 0)
    return pl.pallas_call(
        _mlstm_proj_body,
        grid=(t // tm,),
        in_specs=[pl.BlockSpec((tm, D_MODEL), row),
                  pl.BlockSpec(g.shape, const),
                  pl.BlockSpec((D_MODEL, n), const, pipeline_mode=pl.Buffered(1)),
                  pl.BlockSpec(wg.shape, const)],
        out_specs=[pl.BlockSpec((tm, MLSTM_U), row), pl.BlockSpec((tm, LANES), row)],
        out_shape=[jax.ShapeDtypeStruct((t, MLSTM_U), BF16), jax.ShapeDtypeStruct((t, LANES), F32)],
        scratch_shapes=[pltpu.VMEM((tm, D_MODEL), BF16)],
        compiler_params=_params("parallel"),
        name="mlstm_proj",
    )(x, g, w, wg)


def _mlstm_chunk_body(q_ref, k_ref, v_ref, og_ref, g_ref, bg_ref, a_ref, c_scr, n_scr, m_scr):
    L = L_MLSTM
    H, DK, DV = MLSTM_HEADS, MLSTM_DK, MLSTM_DV

    @pl.when(pl.program_id(1) == 0)
    def _():
        c_scr[...] = jnp.zeros_like(c_scr)
        n_scr[...] = jnp.zeros_like(n_scr)
        m_scr[...] = jnp.zeros_like(m_scr)

    gates = g_ref[...] + bg_ref[...]
    lf = jnp.minimum(gates, 0.0) - jnp.log(1.0 + jnp.exp(-jnp.abs(gates)))
    r_i = lax.broadcasted_iota(jnp.int32, (L, L), 0)
    c_i = lax.broadcasted_iota(jnp.int32, (L, L), 1)
    causal = c_i <= r_i
    tril = causal.astype(BF16)
    hi = lf.astype(BF16)
    r1 = lf - hi.astype(F32)
    mid = r1.astype(BF16)
    lo = (r1 - mid.astype(F32)).astype(BF16)
    b_all = _dot(tril, hi) + _dot(tril, mid) + _dot(tril, lo)
    b_t = b_all.T
    g_t = gates.T

    for h in range(H):
        bc = b_all[:, H + h:H + h + 1]
        igc = gates[:, h:h + 1]
        br = b_t[H + h:H + h + 1, :]
        igr = g_t[h:h + 1, :]
        m = m_scr[h][0:1, 0:1]
        q = q_ref[:, h * DK:(h + 1) * DK]
        k = k_ref[:, h * DK:(h + 1) * DK]
        v = v_ref[:, h * DV:(h + 1) * DV]

        d_log = jnp.where(causal, bc - br + igr, NEG)
        gm = bc + m
        m_t = jnp.maximum(gm, jnp.max(d_log, axis=1, keepdims=True))
        w_intra = jnp.exp(d_log - m_t)
        w_inter = jnp.exp(gm - m_t)
        s = _dot_nt(q, k) * w_intra
        c_old = c_scr[h]
        n_old = n_scr[h][0:1, :]
        num = w_inter * _dot(q, c_old.astype(BF16)) + _dot(s.astype(BF16), v)
        den = (w_inter * jnp.sum(q.astype(F32) * n_old, axis=1, keepdims=True)
               + jnp.sum(s, axis=1, keepdims=True))
        hout = num / jnp.maximum(jnp.abs(den), jnp.exp(-m_t))

        b_last = bc[L - 1:L, :]
        d_state = b_last - bc + igc
        m_new = jnp.maximum(b_last + m, jnp.max(d_state, axis=0, keepdims=True))
        w_s = jnp.exp(d_state - m_new)
        decay = jnp.exp(b_last + m - m_new)
        kw = k.astype(F32) * w_s
        c_scr[h] = decay * c_old + _dot(kw.T.astype(BF16), v)
        n_scr[h] = jnp.broadcast_to(decay * n_old + jnp.sum(kw, axis=0, keepdims=True), (8, DK))
        m_scr[h] = jnp.broadcast_to(m_new, (8, LANES))

        sl = slice(h * DV, (h + 1) * DV)
        a_ref[:, sl] = (og_ref[:, sl].astype(F32) * hout).astype(BF16)


def _mlstm_chunk(u, gates, b_gates, batch, seq):
    t = u.shape[0]
    L = L_MLSTM
    nc = seq // L
    qk = MLSTM_HEADS * MLSTM_DK
    row = lambda col: (lambda b, c: (b * nc + c, col))
    return pl.pallas_call(
        _mlstm_chunk_body,
        grid=(batch, nc),
        in_specs=[pl.BlockSpec((L, qk), row(0)), pl.BlockSpec((L, qk), row(1)),
                  pl.BlockSpec((L, MLSTM_INNER), row(1)), pl.BlockSpec((L, MLSTM_INNER), row(2)),
                  pl.BlockSpec((L, LANES), row(0)),
                  pl.BlockSpec((1, LANES), lambda b, c: (0, 0))],
        out_specs=pl.BlockSpec((L, MLSTM_INNER), row(0)),
        out_shape=jax.ShapeDtypeStruct((t, MLSTM_INNER), BF16),
        scratch_shapes=[pltpu.VMEM((MLSTM_HEADS, MLSTM_DK, MLSTM_DV), F32),
                        pltpu.VMEM((MLSTM_HEADS, 8, MLSTM_DK), F32),
                        pltpu.VMEM((MLSTM_HEADS, 8, LANES), F32)],
        compiler_params=_params("parallel", "arbitrary"),
        name="mlstm_chunk",
    )(u, u, u, u, gates, b_gates)


def _mla_weights(w_in, w_q_b, w_kv_b):
    half = MLA_ROPE // 2
    c_q = w_in[:, :MLA_Q_LORA]
    c_kv = w_in[:, MLA_Q_LORA:MLA_Q_LORA + MLA_KV_LORA]
    r0 = MLA_Q_LORA + MLA_KV_LORA
    k1 = w_in[:, r0:r0 + half]
    k2 = w_in[:, r0 + half:r0 + MLA_ROPE]
    z = w_in[:, r0 + MLA_ROPE:]
    pad = jnp.zeros((D_MODEL, LANES - 4 * half), w_in.dtype)
    win = jnp.concatenate([c_kv, c_q, pad, k1, k2, k2, k1, z], axis=1).astype(BF16)

    wq = w_q_b.reshape(MLA_Q_LORA, MLA_HEADS, MLA_NOPE + MLA_ROPE)
    r1 = wq[..., MLA_NOPE:MLA_NOPE + half]
    r2 = wq[..., MLA_NOPE + half:]
    nope, rot = wq[..., :MLA_NOPE], jnp.concatenate([r1, r2, r2, r1], axis=-1)
    even = jnp.concatenate([nope, rot], axis=-1)[:, 0::2]
    odd = jnp.concatenate([rot, nope], axis=-1)[:, 1::2]
    wq = jnp.stack([even, odd], axis=2).reshape(MLA_Q_LORA, MLA_HEADS * HEAD_PAD).astype(BF16)

    wkv = w_kv_b.reshape(MLA_KV_LORA, MLA_HEADS, MLA_NOPE + MLA_V)
    wk = wkv[..., :MLA_NOPE].reshape(MLA_KV_LORA, MLA_HEADS * MLA_NOPE).astype(BF16)
    wvt =wkv[..., MLA_NOPE:].reshape(MLA_KV_LORA, MLA_HEADS * MLA_V).T.astype(BF16)
    return win, wq, wk, wvt


def _rope_tables(positions):
    half = MLA_ROPE // 2
    inv = ROPE_THETA ** (-jnp.arange(0, MLA_ROPE, 2, dtype=F32) / MLA_ROPE)
    inv_lane = jnp.concatenate([jnp.zeros((MLA_NOPE,), F32), jnp.tile(inv, 4)])
    ang = positions.astype(F32).reshape(-1)[:, None] * inv_lane[None, :]
    grp = ((jnp.arange(LANES) - MLA_NOPE) // half)[None, :]
    sin = jnp.sin(ang)
    rot = jnp.where(grp < 2, jnp.cos(ang), jnp.where(grp == 2, -sin, sin))
    scale = (MLA_NOPE + MLA_ROPE) ** -0.5 * LOG2E
    tq = jnp.where(grp < 0, 1.0, rot) * scale
    tk = jnp.where(grp < 0, 0.0, rot)
    return tq, tk


def kernel(x, p, positions, norm_g, mla_w_in, mla_q_norm, mla_w_q_b, mla_kv_norm, mla_w_kv_b, mla_w_out,
           conv_w_in, conv_w, conv_w_out, mlstm_w_in, mlstm_b_gates, mlstm_w_out, ple_proj, ple_gate,
           final_norm):
    batch, seq, d = x.shape
    t = batch * seq
    xs = x.reshape(t, d)
    p_all = p.reshape(DEPTH, t, PLE_DIM)
    tq_tab, tk_tab = _rope_tables(positions)
    fn =final_norm.reshape(1, d)
    for i in range(DEPTH):
        kind, j = i % 3, i // 3
        g = norm_g[i].reshape(1, d)
        if kind == 0:
            win, wq, wk, wvt = _mla_weights(mla_w_in[j], mla_w_q_b[j], mla_w_kv_b[j])
            q, k, vt, zs = _mla_proj(xs, g, win, mla_q_norm[j].reshape(1, -1), wq,
                                     mla_kv_norm[j].reshape(1, -1), wk, wvt, tq_tab, tk_tab)
            a = _mla_attn(q, k, vt, zs, batch, seq)
            w_out = mla_w_out[j]
        elif kind == 1:
            a = _conv_mix(xs, g, conv_w_in[j].astype(BF16), conv_w[j], batch, seq)
            w_out = conv_w_out[j]
        else:
            w = mlstm_w_in[j]
            n_main = w.shape[1] - 2 * MLSTM_HEADS
            wg = jnp.pad(w[:, n_main:], ((0, 0), (0, LANES - 2 * MLSTM_HEADS))).astype(BF16)
            bg = jnp.pad(mlstm_b_gates[j], (0, LANES - 2 * MLSTM_HEADS)).reshape(1, LANES)
            u, gates = _mlstm_proj(xs, g, w.astype(BF16), wg)
            a = _mlstm_chunk(u, gates, bg, batch, seq)
            w_out = mlstm_w_out[j]
        xs = _out_ple(a, w_out.astype(BF16), xs, p_all, i, ple_proj[i].astype(BF16),
                      ple_gate[i].astype(BF16), fn, final=(i == DEPTH - 1))
    return xs.reshape(batch, seq, d)
```

```python
import functools

import jax
import jax.numpy as jnp
from jax import lax
from jax.experimental import pallas as pl
from jax.experimental.pallas import tpu as pltpu

F32 = jnp.float32
BF16 = jnp.bfloat16

D_MODEL = 1024
DEPTH = 4
CHUNK = 64
PLE_DIM = 256
EPS = 1e-6
MLA_HEADS = 16
MLA_Q_LORA = 384
MLA_KV_LORA = 256
MLA_NOPE = 64
MLA_ROPE = 32
MLA_V = 64
ROPE_THETA = 10000.0
CONV_DIM = D_MODEL
MLSTM_HEADS = 4
MLSTM_INNER = 2 * D_MODEL
MLSTM_DV = MLSTM_INNER // MLSTM_HEADS
MLSTM_DK = MLSTM_DV // 2
MLSTM_U = 2 * MLSTM_HEADS * MLSTM_DK + 2 * MLSTM_INNER

LANES = 128
MXU_N = 256
HEAD_PAD = 128
NEG = -1e30
LOG2E = 1.4426950408889634
VMEM_LIMIT = 48 * 1024 * 1024

TM_OUT = 1024
TM_MLA = 512
MLA_ROWS = 256
TQ = 256
ATTN_HEADS = 4
ATTN_LAG = 5
TM_CONV = 512
TM_MLSTM = 512
TN_MLSTM = 1024
L_MLSTM = 256


def _rms(x):
    return x * lax.rsqrt(jnp.mean(x * x, axis=-1, keepdims=True) + EPS)


def _dot(a, b):
    return jnp.dot(a, b, preferred_element_type=F32)


def _dot_nt(a, b):
    return lax.dot_general(a, b, (((1,), (1,)), ((), ())), preferred_element_type=F32)


def _silu(z):
    return z * jax.nn.sigmoid(z)


def _params(*sem, flags=None):
    return pltpu.CompilerParams(dimension_semantics=sem, vmem_limit_bytes=VMEM_LIMIT, flags=flags)


def _out_ple_body(a_ref, w_ref, x_ref, p_ref, pp_ref, pg_ref, fn_ref, o_ref, *, final):
    x1 = x_ref[...] + _dot(a_ref[...], w_ref[...])
    gate = jax.nn.sigmoid(_dot(_rms(x1).astype(BF16), pg_ref[...]))
    x2 = x1 + gate * _dot(p_ref[...].astype(BF16), pp_ref[...])
    if final:
        x2 = _rms(x2) * fn_ref[...]
    o_ref[...] = x2


def _out_ple(a, w_out, x, p_all, layer, ple_proj, ple_gate, final_norm, final):
    t, k = a.shape
    tm = TM_OUT
    const = lambda i: (0, 0)
    return pl.pallas_call(
        functools.partial(_out_ple_body, final=final),
        grid=(t // tm,),
        in_specs=[
            pl.BlockSpec((tm, k), lambda i: (i, 0)),
            pl.BlockSpec((k, D_MODEL), const),
            pl.BlockSpec((tm, D_MODEL), lambda i: (i, 0)),
            pl.BlockSpec((None, tm, PLE_DIM), lambda i: (layer, i, 0)),
            pl.BlockSpec((PLE_DIM, D_MODEL), const),
            pl.BlockSpec((D_MODEL, D_MODEL), const),
            pl.BlockSpec((1, D_MODEL), const),
        ],
        out_specs=pl.BlockSpec((tm, D_MODEL), lambda i: (i, 0)),
        out_shape=jax.ShapeDtypeStruct((t, D_MODEL), F32),
        compiler_params=_params("parallel"),
        name="out_ple",
    )(a, w_out, x, p_all, ple_proj, ple_gate, final_norm)


def _mla_proj_body(x_ref, g_ref, win_ref, qn_ref, wq_ref, kvn_ref, wk_ref, wvt_ref,
                   tq_ref, tk_ref, q_ref, k_ref, vt_ref, zs_ref):
    z0 = MLA_KV_LORA + 512
    groups = [slice(r, r + MLA_ROWS) for r in range(0, x_ref.shape[0], MLA_ROWS)]
    st = [dict() for _ in groups]

    def stage_norm(n):
        rows = groups[n]
        st[n]["h"] = (_rms(x_ref[rows, :]) * g_ref[...]).astype(BF16)

    def stage_latents(n):
        rows, h = groups[n], st[n]["h"]
        ckv = _dot(h, win_ref[:, 0:MLA_KV_LORA])
        u2 = _dot(h, win_ref[:, MLA_KV_LORA:MLA_KV_LORA + 512])
        cq = u2[:, 0:MLA_Q_LORA]
        kr = u2[:, MLA_Q_LORA:MLA_Q_LORA + LANES]
        st[n]["cqn"] = (_rms(cq) * qn_ref[...]).astype(BF16)
        st[n]["ckvn"] = (_rms(ckv) * kvn_ref[...]).astype(BF16)
        lane = lax.broadcasted_iota(jnp.int32, kr.shape, 1)
        quarter = MLA_ROPE
        g4 = kr * tk_ref[rows, :]
        kr_hi = g4 + jnp.where(lane < MLA_NOPE + quarter, pltpu.roll(g4, LANES - quarter, 1),
                               pltpu.roll(g4, quarter, 1))
        st[n]["kr_hi"] = kr_hi
        st[n]["kr_lo"] = pltpu.roll(kr_hi, MLA_NOPE, 1)
        st[n]["low"] = lane < MLA_NOPE

    def stage_gate(n):
        rows, h = groups[n], st[n]["h"]
        for j in range(MLA_HEADS * MLA_V // MXU_N):
            sl = slice(j * MXU_N, (j + 1) * MXU_N)
            z = _dot(h, win_ref[:, z0 + j * MXU_N:z0 + (j + 1) * MXU_N])
            zs_ref[rows, sl] = _silu(z).astype(BF16)

    def stage_heads(n):
        rows, g = groups[n], st[n]
        tq = tq_ref[rows, :]
        tq2 = jnp.concatenate([tq, pltpu.roll(tq, MLA_NOPE, 1)], axis=1)
        for j in range(MLA_HEADS * HEAD_PAD // MXU_N):
            sl = slice(j * MXU_N, (j + 1) * MXU_N)
            q_ref[rows, sl] = (_dot(g["cqn"], wq_ref[:, sl]) * tq2).astype(BF16)
        for c in range(MLA_HEADS * MLA_NOPE // MXU_N):
            kn = _dot(g["ckvn"], wk_ref[:, c * MXU_N:(c + 1) * MXU_N])
            for half in range(2):
                pair = kn[:, half * LANES:(half + 1) * LANES]
                col = (2 * c + half) * MXU_N
                k_ref[rows, col:col + LANES] = jnp.where(g["low"], pair, g["kr_hi"]).astype(BF16)
                k_ref[rows, col + LANES:col + MXU_N] = jnp.where(g["low"], g["kr_lo"], pair).astype(BF16)
        for j in range(MLA_HEADS * MLA_V // MXU_N):
            sl = slice(j * MXU_N, (j + 1) * MXU_N)
            vt_ref[sl, rows] = _dot_nt(wvt_ref[sl, :], g["ckvn"]).astype(BF16)
        st[n] = None

    stage_norm(0)
    stage_latents(0)
    for n in range(len(groups)):
        if n + 1 < len(groups):
            stage_norm(n + 1)
        stage_gate(n)
        if n + 1 < len(groups):
            stage_latents(n + 1)
        stage_heads(n)


def _mla_proj(x, g, win, qn, wq, kvn, wk, wvt, tq_tab, tk_tab):
    t = x.shape[0]
    tm = TM_MLA
    hw = MLA_HEADS * HEAD_PAD
    vw = MLA_HEADS * MLA_V
    const = lambda i: (0, 0)
    row = lambda i: (i, 0)
    full = lambda arr: pl.BlockSpec(arr.shape, const)
    return pl.pallas_call(
        _mla_proj_body,
        grid=(t // tm,),
        in_specs=[pl.BlockSpec((tm, D_MODEL), row), full(g), full(win), full(qn), full(wq), full(kvn),
                  full(wk), full(wvt),
                  pl.BlockSpec((tm, LANES), row), pl.BlockSpec((tm, LANES), row)],
        out_specs=[pl.BlockSpec((tm, hw), row), pl.BlockSpec((tm, hw), row),
                   pl.BlockSpec((vw, tm), lambda i: (0, i)), pl.BlockSpec((tm, vw), row)],
        out_shape=[jax.ShapeDtypeStruct((t, hw), BF16), jax.ShapeDtypeStruct((t, hw), BF16),
                   jax.ShapeDtypeStruct((vw, t), BF16), jax.ShapeDtypeStruct((t, vw), BF16)],
        compiler_params=_params("parallel"),
        name="mla_proj",
    )(x, g, win, qn, wq, kvn, wk, wvt, tq_tab, tk_tab)


def _mla_attn_body(q_ref, k_ref, vt_ref, zs_ref, a_ref):
    seq = q_ref.shape[0]
    nq = seq // TQ
    key_c = lax.broadcasted_iota(jnp.int32, (TQ, TQ), 0) // CHUNK
    qry_c = lax.broadcasted_iota(jnp.int32, (TQ, TQ), 1) // CHUNK
    diag_ok = key_c <= qry_c
    units = [(i, hh, j) for i in range(nq) for hh in range(ATTN_HEADS) for j in range(i + 1)]
    lead = nq + 2
    lag = ATTN_LAG
    scores, probs = {}, {}
    m_run, l_run, acc = {}, {}, {}
    outs = {}

    def score_unit(u):
        i, hh, j = units[u]
        hs = slice(hh * HEAD_PAD, (hh + 1) * HEAD_PAD)
        s = _dot_nt(k_ref[j * TQ:(j + 1) * TQ, hs], q_ref[i * TQ:(i + 1) * TQ, hs])
        if j == i:
            s = jnp.where(diag_ok, s, NEG)
        mj = jnp.max(s, axis=0, keepdims=True)
        m_run[i, hh] = mj if j == 0 else jnp.maximum(m_run[i, hh], mj)
        scores[u] = s

    def exp_unit(u):
        i, hh, j = units[u]
        p = jnp.exp2(scores.pop(u) - m_run[i, hh])
        lj = jnp.sum(p, axis=0, keepdims=True)
        l_run[i, hh] = lj if j == 0 else l_run[i, hh] + lj
        probs[u] = p.astype(BF16)

    def value_unit(u):
        i, hh, j = units[u]
        oj = _dot(vt_ref[hh * MLA_V:(hh + 1) * MLA_V, j * TQ:(j + 1) * TQ], probs.pop(u))
        acc[i, hh] = oj if j == 0 else acc[i, hh] + oj

    def finish_unit(u):
        i, hh, j = units[u]
        if j != i:
            return
        outs[hh] = acc.pop((i, hh)) * (1.0 / l_run.pop((i, hh)))
        if hh % 2 == 1:
            rows = slice(i * TQ, (i + 1) * TQ)
            cols = slice((hh - 1) * MLA_V, (hh + 1) * MLA_V)
            o = jnp.concatenate([outs.pop(hh - 1), outs.pop(hh)], axis=0).T
            a_ref[rows, cols] = (o * zs_ref[rows, cols].astype(F32)).astype(BF16)

    n_units = len(units)
    for t in range(-lead, n_units + lag + 3):
        if 0 <= t + lead < n_units:
            score_unit(t + lead)
        if 0 <= t - lag < n_units:
            value_unit(t - lag)
        if 0 <= t < n_units:
            exp_unit(t)
        if 0 <= t - lag - 2 < n_units:
            finish_unit(t - lag - 2)


def _mla_attn(q, k, vt, zs, batch, seq):
    t = q.shape[0]
    nh = ATTN_HEADS
    blk = lambda w: pl.BlockSpec((seq, w), lambda b, h: (b, h))
    return pl.pallas_call(
        _mla_attn_body,
        grid=(batch, MLA_HEADS // nh),
        in_specs=[blk(nh * HEAD_PAD), blk(nh * HEAD_PAD),
                  pl.BlockSpec((nh * MLA_V, seq), lambda b, h: (h, b)), blk(nh * MLA_V)],
        out_specs=blk(nh * MLA_V),
        out_shape=jax.ShapeDtypeStruct((t, MLA_HEADS * MLA_V), BF16),
        compiler_params=_params("parallel", "parallel"),
        name="mla_attn",
    )(q, k, vt, zs)


def _conv_body(x_ref, g_ref, win_ref, cw_ref, a_ref, ext_ref):
    tm = TM_CONV
    t = pl.program_id(1)
    h = (_rms(x_ref[...]) * g_ref[...]).astype(BF16)

    @pl.when(t == 0)
    def _():
        ext_ref[0:8, :] = jnp.zeros((8, CONV_DIM), F32)

    @pl.when(t > 0)
    def _():
        ext_ref[0:8, :] = ext_ref[tm:tm + 8, :]

    for c in range(CONV_DIM // MXU_N):
        sl = slice(c * MXU_N, (c + 1) * MXU_N)
        col = lambda part: slice(part * CONV_DIM + c * MXU_N, part * CONV_DIM + (c + 1) * MXU_N)
        prod = _dot(h, win_ref[:, col(1)]) * _dot(h, win_ref[:, col(2)])
        ext_ref[8:8 + tm, sl] = prod
        y = (cw_ref[2:3, sl] * prod + cw_ref[1:2, sl] * ext_ref[7:7 + tm, sl]
             + cw_ref[0:1, sl] * ext_ref[6:6 + tm, sl])
        bg = _dot(h, win_ref[:, col(0)])
        z = _dot(h, win_ref[:, col(3)])
        a_ref[:, sl] = (bg * y * _silu(z)).astype(BF16)


def _conv_mix(x, g, win, cw, batch, seq):
    t = x.shape[0]
    tm = TM_CONV
    nt = seq // tm
    const = lambda b, i: (0, 0)
    row = lambda b, i: (b * nt + i, 0)
    return pl.pallas_call(
        _conv_body,
        grid=(batch, nt),
        in_specs=[pl.BlockSpec((tm, D_MODEL), row), pl.BlockSpec(g.shape, const),
                  pl.BlockSpec(win.shape, const), pl.BlockSpec(cw.shape, const)],
        out_specs=pl.BlockSpec((tm, CONV_DIM), row),
        out_shape=jax.ShapeDtypeStruct((t, CONV_DIM), BF16),
        scratch_shapes=[pltpu.VMEM((tm + 8, CONV_DIM), F32)],
        compiler_params=_params("parallel", "arbitrary"),
        name="conv_mix",
    )(x, g, win, cw)


def _mlstm_proj_body(x_ref, g_ref, w_ref, wg_ref, u_ref, gates_ref, h_scr):
    h_scr[...] = (_rms(x_ref[...]) * g_ref[...]).astype(BF16)
    gates_ref[...] = _dot(h_scr[...], wg_ref[...])
    tn = TN_MLSTM
    qk = MLSTM_HEADS * MLSTM_DK
    cols = lambda start, j: slice(start + j * tn, start + (j + 1) * tn)
    proj = lambda sl: _dot(h_scr[...], w_ref[:, sl])
    for j in range(qk // tn):
        u_ref[:, cols(0, j)] = proj(cols(0, j)).astype(BF16)
        u_ref[:, cols(qk, j)] = (proj(cols(qk, j)) * (MLSTM_DK ** -0.5)).astype(BF16)
    for j in range(MLSTM_INNER // tn):
        u_ref[:, cols(2 * qk, j)] = proj(cols(2 * qk, j)).astype(BF16)
        o = proj(cols(2 * qk + MLSTM_INNER, j))
        z = proj(cols(2 * qk + 2 * MLSTM_INNER, j))
        u_ref[:, cols(2 * qk + MLSTM_INNER, j)] = (jax.nn.sigmoid(o) * _silu(z)).astype(BF16)


def _mlstm_proj(x, g, w, wg):
    t = x.shape[0]
    n = w.shape[1] - 2 * MLSTM_HEADS
    tm = TM_MLSTM
    const = lambda i: (0, 0)
    row = lambda i: (i, 0)
    return pl.pallas_call(
        _mlstm_proj_body,
        grid=(t // tm,),
        in_specs=[pl.BlockSpec((tm, D_MODEL), row),
                  pl.BlockSpec(g.shape, const),
                  pl.BlockSpec((D_MODEL, n), const, pipeline_mode=pl.Buffered(1)),
                  pl.BlockSpec(wg.shape, const)],
        out_specs=[pl.BlockSpec((tm, MLSTM_U), row), pl.BlockSpec((tm, LANES), row)],
        out_shape=[jax.ShapeDtypeStruct((t, MLSTM_U), BF16), jax.ShapeDtypeStruct((t, LANES), F32)],
        scratch_shapes=[pltpu.VMEM((tm, D_MODEL), BF16)],
        compiler_params=_params("parallel"),
        name="mlstm_proj",
    )(x, g, w, wg)


def _mlstm_chunk_body(q_ref, k_ref, v_ref, og_ref, g_ref, bg_ref, a_ref, c_scr, n_scr, m_scr):
    L = L_MLSTM
    H, DK, DV = MLSTM_HEADS, MLSTM_DK, MLSTM_DV

    @pl.when(pl.program_id(1) == 0)
    def _():
        c_scr[...] = jnp.zeros_like(c_scr)
        n_scr[...] = jnp.zeros_like(n_scr)
        m_scr[...] = jnp.zeros_like(m_scr)

    gates = g_ref[...] + bg_ref[...]
    lf = jnp.minimum(gates, 0.0) - jnp.log(1.0 + jnp.exp(-jnp.abs(gates)))
    r_i = lax.broadcasted_iota(jnp.int32, (L, L), 0)
    c_i = lax.broadcasted_iota(jnp.int32, (L, L), 1)
    causal = c_i <= r_i
    tril = causal.astype(BF16)
    hi = lf.astype(BF16)
    r1 = lf - hi.astype(F32)
    mid = r1.astype(BF16)
    lo = (r1 - mid.astype(F32)).astype(BF16)
    b_all = _dot(tril, hi) + _dot(tril, mid) + _dot(tril, lo)
    b_t = b_all.T
    g_t = gates.T

    for h in range(H):
        bc = b_all[:, H + h:H + h + 1]
        igc = gates[:, h:h + 1]
        br = b_t[H + h:H + h + 1, :]
        igr = g_t[h:h + 1, :]
        m = m_scr[h][0:1, 0:1]
        q = q_ref[:, h * DK:(h + 1) * DK]
        k = k_ref[:, h * DK:(h + 1) * DK]
        v = v_ref[:, h * DV:(h + 1) * DV]

        d_log = jnp.where(causal, bc - br + igr, NEG)
        gm = bc + m
        m_t = jnp.maximum(gm, jnp.max(d_log, axis=1, keepdims=True))
        w_intra = jnp.exp(d_log - m_t)
        w_inter = jnp.exp(gm - m_t)
        s = _dot_nt(q, k) * w_intra
        c_old = c_scr[h]
        n_old = n_scr[h][0:1, :]
        num = w_inter * _dot(q, c_old.astype(BF16)) + _dot(s.astype(BF16), v)
        den = (w_inter * jnp.sum(q.astype(F32) * n_old, axis=1, keepdims=True)
               + jnp.sum(s, axis=1, keepdims=True))
        hout = num / jnp.maximum(jnp.abs(den), jnp.exp(-m_t))

        b_last = bc[L - 1:L, :]
        d_state = b_last - bc + igc
        m_new = jnp.maximum(b_last + m, jnp.max(d_state, axis=0, keepdims=True))
        w_s = jnp.exp(d_state - m_new)
        decay = jnp.exp(b_last + m - m_new)
        kw = k.astype(F32) * w_s
        c_scr[h] = decay * c_old + _dot(kw.T.astype(BF16), v)
        n_scr[h] = jnp.broadcast_to(decay * n_old + jnp.sum(kw, axis=0, keepdims=True), (8, DK))
        m_scr[h] = jnp.broadcast_to(m_new, (8, LANES))

        sl = slice(h * DV, (h + 1) * DV)
        a_ref[:, sl] = (og_ref[:, sl].astype(F32) * hout).astype(BF16)


def _mlstm_chunk(u, gates, b_gates, batch, seq):
    t = u.shape[0]
    L = L_MLSTM
    nc = seq // L
    qk = MLSTM_HEADS * MLSTM_DK
    row = lambda col: (lambda b, c: (b * nc + c, col))
    return pl.pallas_call(
        _mlstm_chunk_body,
        grid=(batch, nc),
        in_specs=[pl.BlockSpec((L, qk), row(0)), pl.BlockSpec((L, qk), row(1)),
                  pl.BlockSpec((L, MLSTM_INNER), row(1)), pl.BlockSpec((L, MLSTM_INNER), row(2)),
                  pl.BlockSpec((L, LANES), row(0)),
                  pl.BlockSpec((1, LANES), lambda b, c: (0, 0))],
        out_specs=pl.BlockSpec((L, MLSTM_INNER), row(0)),
        out_shape=jax.ShapeDtypeStruct((t, MLSTM_INNER), BF16),
        scratch_shapes=[pltpu.VMEM((MLSTM_HEADS, MLSTM_DK, MLSTM_DV), F32),
                        pltpu.VMEM((MLSTM_HEADS, 8, MLSTM_DK), F32),
                        pltpu.VMEM((MLSTM_HEADS, 8, LANES), F32)],
        compiler_params=_params("parallel", "arbitrary"),
        name="mlstm_chunk",
    )(u, u, u, u, gates, b_gates)


def _mla_weights(w_in, w_q_b, w_kv_b):
    half = MLA_ROPE // 2
    c_q = w_in[:, :MLA_Q_LORA]
    c_kv = w_in[:, MLA_Q_LORA:MLA_Q_LORA + MLA_KV_LORA]
    r0 = MLA_Q_LORA + MLA_KV_LORA
    k1 = w_in[:, r0:r0 + half]
    k2 = w_in[:, r0 + half:r0 + MLA_ROPE]
    z = w_in[:, r0 + MLA_ROPE:]
    pad = jnp.zeros((D_MODEL, LANES - 4 * half), w_in.dtype)
    win = jnp.concatenate([c_kv, c_q, pad, k1, k2, k2, k1, z], axis=1).astype(BF16)

    wq = w_q_b.reshape(MLA_Q_LORA, MLA_HEADS, MLA_NOPE + MLA_ROPE)
    r1 = wq[..., MLA_NOPE:MLA_NOPE + half]
    r2 = wq[..., MLA_NOPE + half:]
    nope, rot = wq[..., :MLA_NOPE], jnp.concatenate([r1, r2, r2, r1], axis=-1)
    even = jnp.concatenate([nope, rot], axis=-1)[:, 0::2]
    odd = jnp.concatenate([rot, nope], axis=-1)[:, 1::2]
    wq = jnp.stack([even, odd], axis=2).reshape(MLA_Q_LORA, MLA_HEADS * HEAD_PAD).astype(BF16)

    wkv = w_kv_b.reshape(MLA_KV_LORA, MLA_HEADS, MLA_NOPE + MLA_V)
    wk = wkv[..., :MLA_NOPE].reshape(MLA_KV_LORA, MLA_HEADS * MLA_NOPE).astype(BF16)
    wvt =wkv[..., MLA_NOPE:].reshape(MLA_KV_LORA, MLA_HEADS * MLA_V).T.astype(BF16)
    return win, wq, wk, wvt


def _rope_tables(positions):
    half = MLA_ROPE // 2
    inv = ROPE_THETA ** (-jnp.arange(0, MLA_ROPE, 2, dtype=F32) / MLA_ROPE)
    inv_lane = jnp.concatenate([jnp.zeros((MLA_NOPE,), F32), jnp.tile(inv, 4)])
    ang = positions.astype(F32).reshape(-1)[:, None] * inv_lane[None, :]
    grp = ((jnp.arange(LANES) - MLA_NOPE) // half)[None, :]
    sin = jnp.sin(ang)
    rot = jnp.where(grp < 2, jnp.cos(ang), jnp.where(grp == 2, -sin, sin))
    scale = (MLA_NOPE + MLA_ROPE) ** -0.5 * LOG2E
    tq = jnp.where(grp < 0, 1.0, rot) * scale
    tk = jnp.where(grp < 0, 0.0, rot)
    return tq, tk


def kernel(x, p, positions, norm_g, mla_w_in, mla_q_norm, mla_w_q_b, mla_kv_norm, mla_w_kv_b, mla_w_out,
           conv_w_in, conv_w, conv_w_out, mlstm_w_in, mlstm_b_gates, mlstm_w_out, ple_proj, ple_gate,
           final_norm):
    batch, seq, d = x.shape
    t = batch * seq
    xs = x.reshape(t, d)
    p_all = p.reshape(DEPTH, t, PLE_DIM)
    tq_tab, tk_tab = _rope_tables(positions)
    fn =final_norm.reshape(1, d)
    for i in range(DEPTH):
        kind, j = i % 3, i // 3
        g = norm_g[i].reshape(1, d)
        if kind == 0:
            win, wq, wk, wvt = _mla_weights(mla_w_in[j], mla_w_q_b[j], mla_w_kv_b[j])
            q, k, vt, zs = _mla_proj(xs, g, win, mla_q_norm[j].reshape(1, -1), wq,
                                     mla_kv_norm[j].reshape(1, -1), wk, wvt, tq_tab, tk_tab)
            a = _mla_attn(q, k, vt, zs, batch, seq)
            w_out = mla_w_out[j]
        elif kind == 1:
            a = _conv_mix(xs, g, conv_w_in[j].astype(BF16), conv_w[j], batch, seq)
            w_out = conv_w_out[j]
        else:
            w = mlstm_w_in[j]
            n_main = w.shape[1] - 2 * MLSTM_HEADS
            wg = jnp.pad(w[:, n_main:], ((0, 0), (0, LANES - 2 * MLSTM_HEADS))).astype(BF16)
            bg = jnp.pad(mlstm_b_gates[j], (0, LANES - 2 * MLSTM_HEADS)).reshape(1, LANES)
            u, gates = _mlstm_proj(xs, g, w.astype(BF16), wg)
            a = _mlstm_chunk(u, gates, bg, batch, seq)
            w_out = mlstm_w_out[j]
        xs = _out_ple(a, w_out.astype(BF16), xs, p_all, i, ple_proj[i].astype(BF16),
                      ple_gate[i].astype(BF16), fn, final=(i == DEPTH - 1))
    return xs.reshape(batch, seq, d)
```

```python
import functools

import jax
import jax.numpy as jnp
from jax import lax
from jax.experimental import pallas as pl
from jax.experimental.pallas import tpu as pltpu

F32 = jnp.float32
BF16 = jnp.bfloat16

D_MODEL = 1024
DEPTH = 4
CHUNK = 64
PLE_DIM = 256
EPS = 1e-6
MLA_HEADS = 16
MLA_Q_LORA = 384
MLA_KV_LORA = 256
MLA_NOPE = 64
MLA_ROPE = 32
MLA_V = 64
ROPE_THETA = 10000.0
CONV_DIM = D_MODEL
MLSTM_HEADS = 4
MLSTM_INNER = 2 * D_MODEL
MLSTM_DV = MLSTM_INNER // MLSTM_HEADS
MLSTM_DK = MLSTM_DV // 2
MLSTM_U = 2 * MLSTM_HEADS * MLSTM_DK + 2 * MLSTM_INNER

LANES = 128
MXU_N = 256
HEAD_PAD = 128
NEG = -1e30
LOG2E = 1.4426950408889634
VMEM_LIMIT = 48 * 1024 * 1024

TM_OUT = 1024
TM_MLA = 512
MLA_ROWS = 256
TQ = 256
ATTN_HEADS = 4
ATTN_LAG = 5
TM_CONV = 512
CONV_ROWS = 256
TM_MLSTM = 512
MLSTM_ROWS = 256
TN_MLSTM = 1024
L_MLSTM = 256


def _rms(x):
    return x * lax.rsqrt(jnp.mean(x * x, axis=-1, keepdims=True) + EPS)


def _dot(a, b):
    return jnp.dot(a, b, preferred_element_type=F32)


def _dot_nt(a, b):
    return lax.dot_general(a, b, (((1,), (1,)), ((), ())), preferred_element_type=F32)


def _silu(z):
    return z * jax.nn.sigmoid(z)


def _params(*sem, flags=None):
    return pltpu.CompilerParams(dimension_semantics=sem, vmem_limit_bytes=VMEM_LIMIT, flags=flags)


def _out_ple_body(a_ref, w_ref, x_ref, p_ref, pp_ref, pg_ref, fn_ref, o_ref, *, final):
    x1 = x_ref[...] + _dot(a_ref[...], w_ref[...])
    gate = jax.nn.sigmoid(_dot(_rms(x1).astype(BF16), pg_ref[...]))
    x2 = x1 + gate * _dot(p_ref[...].astype(BF16), pp_ref[...])
    if final:
        x2 = _rms(x2) * fn_ref[...]
    o_ref[...] = x2


def _out_ple(a, w_out, x, p_all, layer, ple_proj, ple_gate, final_norm, final):
    t, k = a.shape
    tm = TM_OUT
    const = lambda i: (0, 0)
    return pl.pallas_call(
        functools.partial(_out_ple_body, final=final),
        grid=(t // tm,),
        in_specs=[
            pl.BlockSpec((tm, k), lambda i: (i, 0)),
            pl.BlockSpec((k, D_MODEL), const),
            pl.BlockSpec((tm, D_MODEL), lambda i: (i, 0)),
            pl.BlockSpec((None, tm, PLE_DIM), lambda i: (layer, i, 0)),
            pl.BlockSpec((PLE_DIM, D_MODEL), const),
            pl.BlockSpec((D_MODEL, D_MODEL), const),
            pl.BlockSpec((1, D_MODEL), const),
        ],
        out_specs=pl.BlockSpec((tm, D_MODEL), lambda i: (i, 0)),
        out_shape=jax.ShapeDtypeStruct((t, D_MODEL), F32),
        compiler_params=_params("parallel"),
        name="out_ple",
    )(a, w_out, x, p_all, ple_proj, ple_gate, final_norm)


def _mla_proj_body(x_ref, g_ref, win_ref, qn_ref, wq_ref, kvn_ref, wk_ref, wvt_ref,
                   tq_ref, tk_ref, q_ref, k_ref, vt_ref, zs_ref):
    z0 = MLA_KV_LORA + 512
    groups = [slice(r, r + MLA_ROWS) for r in range(0, x_ref.shape[0], MLA_ROWS)]
    st = [dict() for _ in groups]

    def stage_norm(n):
        rows = groups[n]
        st[n]["h"] = (_rms(x_ref[rows, :]) * g_ref[...]).astype(BF16)

    def stage_latents(n):
        rows, h = groups[n], st[n]["h"]
        ckv = _dot(h, win_ref[:, 0:MLA_KV_LORA])
        u2 = _dot(h, win_ref[:, MLA_KV_LORA:MLA_KV_LORA + 512])
        cq = u2[:, 0:MLA_Q_LORA]
        kr = u2[:, MLA_Q_LORA:MLA_Q_LORA + LANES]
        st[n]["cqn"] = (_rms(cq) * qn_ref[...]).astype(BF16)
        st[n]["ckvn"] = (_rms(ckv) * kvn_ref[...]).astype(BF16)
        lane = lax.broadcasted_iota(jnp.int32, kr.shape, 1)
        quarter = MLA_ROPE
        g4 = kr * tk_ref[rows, :]
        kr_hi = g4 + jnp.where(lane < MLA_NOPE + quarter, pltpu.roll(g4, LANES - quarter, 1),
                               pltpu.roll(g4, quarter, 1))
        st[n]["kr_hi"] = kr_hi
        st[n]["kr_lo"] = pltpu.roll(kr_hi, MLA_NOPE, 1)
        st[n]["low"] = lane < MLA_NOPE

    def stage_gate(n):
        rows, h = groups[n], st[n]["h"]
        for j in range(MLA_HEADS * MLA_V // MXU_N):
            sl = slice(j * MXU_N, (j + 1) * MXU_N)
            z = _dot(h, win_ref[:, z0 + j * MXU_N:z0 + (j + 1) * MXU_N])
            zs_ref[rows, sl] = _silu(z).astype(BF16)

    def stage_heads(n):
        rows, g = groups[n], st[n]
        tq = tq_ref[rows, :]
        tq2 = jnp.concatenate([tq, pltpu.roll(tq, MLA_NOPE, 1)], axis=1)
        for j in range(MLA_HEADS * HEAD_PAD // MXU_N):
            sl = slice(j * MXU_N, (j + 1) * MXU_N)
            q_ref[rows, sl] = (_dot(g["cqn"], wq_ref[:, sl]) * tq2).astype(BF16)
        for c in range(MLA_HEADS * MLA_NOPE // MXU_N):
            kn = _dot(g["ckvn"], wk_ref[:, c * MXU_N:(c + 1) * MXU_N])
            for half in range(2):
                pair = kn[:, half * LANES:(half + 1) * LANES]
                col = (2 * c + half) * MXU_N
                k_ref[rows, col:col + LANES] = jnp.where(g["low"], pair, g["kr_hi"]).astype(BF16)
                k_ref[rows, col + LANES:col + MXU_N] = jnp.where(g["low"], g["kr_lo"], pair).astype(BF16)
        for j in range(MLA_HEADS * MLA_V // MXU_N):
            sl = slice(j * MXU_N, (j + 1) * MXU_N)
            vt_ref[sl, rows] = _dot_nt(wvt_ref[sl, :], g["ckvn"]).astype(BF16)
        st[n] = None

    stage_norm(0)
    stage_latents(0)
    for n in range(len(groups)):
        if n + 1 < len(groups):
            stage_norm(n + 1)
        stage_gate(n)
        if n + 1 < len(groups):
            stage_latents(n + 1)
        stage_heads(n)


def _mla_proj(x, g, win, qn, wq, kvn, wk, wvt, tq_tab, tk_tab):
    t = x.shape[0]
    tm = TM_MLA
    hw = MLA_HEADS * HEAD_PAD
    vw = MLA_HEADS * MLA_V
    const = lambda i: (0, 0)
    row = lambda i: (i, 0)
    full = lambda arr: pl.BlockSpec(arr.shape, const)
    return pl.pallas_call(
        _mla_proj_body,
        grid=(t // tm,),
        in_specs=[pl.BlockSpec((tm, D_MODEL), row), full(g), full(win), full(qn), full(wq), full(kvn),
                  full(wk), full(wvt),
                  pl.BlockSpec((tm, LANES), row), pl.BlockSpec((tm, LANES), row)],
        out_specs=[pl.BlockSpec((tm, hw), row), pl.BlockSpec((tm, hw), row),
                   pl.BlockSpec((vw, tm), lambda i: (0, i)), pl.BlockSpec((tm, vw), row)],
        out_shape=[jax.ShapeDtypeStruct((t, hw), BF16), jax.ShapeDtypeStruct((t, hw), BF16),
                   jax.ShapeDtypeStruct((vw, t), BF16), jax.ShapeDtypeStruct((t, vw), BF16)],
        compiler_params=_params("parallel"),
        name="mla_proj",
    )(x, g, win, qn, wq, kvn, wk, wvt, tq_tab, tk_tab)


def _mla_attn_body(q_ref, k_ref, vt_ref, zs_ref, a_ref):
    seq = q_ref.shape[0]
    nq = seq // TQ
    key_c = lax.broadcasted_iota(jnp.int32, (TQ, TQ), 0) // CHUNK
    qry_c = lax.broadcasted_iota(jnp.int32, (TQ, TQ), 1) // CHUNK
    diag_ok = key_c <= qry_c
    units = [(i, hh, j) for i in range(nq) for hh in range(ATTN_HEADS) for j in range(i + 1)]
    lead = nq + 2
    lag = ATTN_LAG
    scores, probs = {}, {}
    m_run, l_run, acc = {}, {}, {}
    outs = {}

    def score_unit(u):
        i, hh, j = units[u]
        hs = slice(hh * HEAD_PAD, (hh + 1) * HEAD_PAD)
        s = _dot_nt(k_ref[j * TQ:(j + 1) * TQ, hs], q_ref[i * TQ:(i + 1) * TQ, hs])
        if j == i:
            s = jnp.where(diag_ok, s, NEG)
        mj = jnp.max(s, axis=0, keepdims=True)
        m_run[i, hh] = mj if j == 0 else jnp.maximum(m_run[i, hh], mj)
        scores[u] = s

    def exp_unit(u):
        i, hh, j = units[u]
        p = jnp.exp2(scores.pop(u) - m_run[i, hh])
        lj = jnp.sum(p, axis=0, keepdims=True)
        l_run[i, hh] = lj if j == 0 else l_run[i, hh] + lj
        probs[u] = p.astype(BF16)

    def value_unit(u):
        i, hh, j = units[u]
        oj = _dot(vt_ref[hh * MLA_V:(hh + 1) * MLA_V, j * TQ:(j + 1) * TQ], probs.pop(u))
        acc[i, hh] = oj if j == 0 else acc[i, hh] + oj

    def finish_unit(u):
        i, hh, j = units[u]
        if j != i:
            return
        outs[hh] = acc.pop((i, hh)) * (1.0 / l_run.pop((i, hh)))
        if hh % 2 == 1:
            rows = slice(i * TQ, (i + 1) * TQ)
            cols = slice((hh - 1) * MLA_V, (hh + 1) * MLA_V)
            o = jnp.concatenate([outs.pop(hh - 1), outs.pop(hh)], axis=0).T
            a_ref[rows, cols] = (o * zs_ref[rows, cols].astype(F32)).astype(BF16)

    n_units = len(units)
    for t in range(-lead, n_units + lag + 3):
        if 0 <= t + lead < n_units:
            score_unit(t + lead)
        if 0 <= t - lag < n_units:
            value_unit(t - lag)
        if 0 <= t < n_units:
            exp_unit(t)
        if 0 <= t - lag - 2 < n_units:
            finish_unit(t - lag - 2)


def _mla_attn(q, k, vt, zs, batch, seq):
    t = q.shape[0]
    nh = ATTN_HEADS
    blk = lambda w: pl.BlockSpec((seq, w), lambda b, h: (b, h))
    return pl.pallas_call(
        _mla_attn_body,
        grid=(batch, MLA_HEADS // nh),
        in_specs=[blk(nh * HEAD_PAD), blk(nh * HEAD_PAD),
                  pl.BlockSpec((nh * MLA_V, seq), lambda b, h: (h, b)), blk(nh * MLA_V)],
        out_specs=blk(nh * MLA_V),
        out_shape=jax.ShapeDtypeStruct((t, MLA_HEADS * MLA_V), BF16),
        compiler_params=_params("parallel", "parallel"),
        name="mla_attn",
    )(q, k, vt, zs)


def _conv_body(x_ref, g_ref, win_ref, cw_ref, a_ref, ext_ref):
    tm = TM_CONV
    t = pl.program_id(1)

    @pl.when(t == 0)
    def _():
        ext_ref[0:8, :] = jnp.zeros((8, CONV_DIM), F32)

    @pl.when(t > 0)
    def _():
        ext_ref[0:8, :] = ext_ref[tm:tm + 8, :]

    norm = lambda r: (_rms(x_ref[r:r + CONV_ROWS, :]) * g_ref[...]).astype(BF16)
    h_next = norm(0)
    for r in range(0, tm, CONV_ROWS):
        h = h_next
        for c in range(CONV_DIM // MXU_N):
            sl = slice(c * MXU_N, (c + 1) * MXU_N)
            col = lambda part: slice(part * CONV_DIM + c * MXU_N, part * CONV_DIM + (c + 1) * MXU_N)
            prod = _dot(h, win_ref[:, col(1)]) * _dot(h, win_ref[:, col(2)])
            ext_ref[8 + r:8 + r + CONV_ROWS, sl] = prod
            y = (cw_ref[2:3, sl] * prod + cw_ref[1:2, sl] * ext_ref[7 + r:7 + r + CONV_ROWS, sl]
                 + cw_ref[0:1, sl] * ext_ref[6 + r:6 + r + CONV_ROWS, sl])
            bg = _dot(h, win_ref[:, col(0)])
            z = _dot(h, win_ref[:, col(3)])
            a_ref[r:r + CONV_ROWS, sl] = (bg * y * _silu(z)).astype(BF16)
            if c == 0 and r + CONV_ROWS < tm:
                h_next = norm(r + CONV_ROWS)


def _conv_mix(x, g, win, cw, batch, seq):
    t = x.shape[0]
    tm = TM_CONV
    nt = seq // tm
    const = lambda b, i: (0, 0)
    row = lambda b, i: (b * nt + i, 0)
    return pl.pallas_call(
        _conv_body,
        grid=(batch, nt),
        in_specs=[pl.BlockSpec((tm, D_MODEL), row), pl.BlockSpec(g.shape, const),
                  pl.BlockSpec(win.shape, const), pl.BlockSpec(cw.shape, const)],
        out_specs=pl.BlockSpec((tm, CONV_DIM), row),
        out_shape=jax.ShapeDtypeStruct((t, CONV_DIM), BF16),
        scratch_shapes=[pltpu.VMEM((tm + 8, CONV_DIM), F32)],
        compiler_params=_params("parallel", "arbitrary"),
        name="conv_mix",
    )(x, g, win, cw)


def _mlstm_proj_body(x_ref, g_ref, w_ref, wg_ref, u_ref, gates_ref):
    tn = TN_MLSTM
    qk = MLSTM_HEADS * MLSTM_DK
    cols = lambda start, j: slice(start + j * tn, start + (j + 1) * tn)
    norm = lambda r: (_rms(x_ref[r:r + MLSTM_ROWS, :]) * g_ref[...]).astype(BF16)
    h_next = norm(0)
    for r in range(0, x_ref.shape[0], MLSTM_ROWS):
        h = h_next
        rows = slice(r, r + MLSTM_ROWS)
        proj = lambda sl: _dot(h, w_ref[:, sl])
        gates_ref[rows, :] = _dot(h, wg_ref[...])
        for j in range(qk // tn):
            u_ref[rows, cols(0, j)] = proj(cols(0, j)).astype(BF16)
            u_ref[rows, cols(qk, j)] = (proj(cols(qk, j)) * (MLSTM_DK ** -0.5)).astype(BF16)
        if r + MLSTM_ROWS < x_ref.shape[0]:
            h_next = norm(r + MLSTM_ROWS)
        for j in range(MLSTM_INNER // tn):
            u_ref[rows, cols(2 * qk, j)] = proj(cols(2 * qk, j)).astype(BF16)
            o = proj(cols(2 * qk + MLSTM_INNER, j))
            z = proj(cols(2 * qk + 2 * MLSTM_INNER, j))
            u_ref[rows, cols(2 * qk + MLSTM_INNER, j)] = (jax.nn.sigmoid(o) * _silu(z)).astype(BF16)


def _mlstm_proj(x, g, w, wg):
    t = x.shape[0]
    n = w.shape[1] - 2 * MLSTM_HEADS
    tm = TM_MLSTM
    const = lambda i: (0, 0)
    row = lambda i: (i, 0)
    return pl.pallas_call(
        _mlstm_proj_body,
        grid=(t // tm,),
        in_specs=[pl.BlockSpec((tm, D_MODEL), row),
                  pl.BlockSpec(g.shape, const),
                  pl.BlockSpec((D_MODEL, n), const, pipeline_mode=pl.Buffered(1)),
                  pl.BlockSpec(wg.shape, const)],
        out_specs=[pl.BlockSpec((tm, MLSTM_U), row), pl.BlockSpec((tm, LANES), row)],
        out_shape=[jax.ShapeDtypeStruct((t, MLSTM_U), BF16), jax.ShapeDtypeStruct((t, LANES), F32)],
        compiler_params=_params("parallel"),
        name="mlstm_proj",
    )(x, g, w, wg)


def _mlstm_chunk_body(q_ref, k_ref, v_ref, og_ref, g_ref, bg_ref, a_ref, c_scr, n_scr, m_scr):
    L = L_MLSTM
    H, DK, DV = MLSTM_HEADS, MLSTM_DK, MLSTM_DV

    @pl.when(pl.program_id(1) == 0)
    def _():
        c_scr[...] = jnp.zeros_like(c_scr)
        n_scr[...] = jnp.zeros_like(n_scr)
        m_scr[...] = jnp.zeros_like(m_scr)

    gates = g_ref[...] + bg_ref[...]
    lf = jnp.minimum(gates, 0.0) - jnp.log(1.0 + jnp.exp(-jnp.abs(gates)))
    r_i = lax.broadcasted_iota(jnp.int32, (L, L), 0)
    c_i = lax.broadcasted_iota(jnp.int32, (L, L), 1)
    causal = c_i <= r_i
    tril = causal.astype(BF16)
    hi = lf.astype(BF16)
    r1 = lf - hi.astype(F32)
    mid = r1.astype(BF16)
    lo = (r1 - mid.astype(F32)).astype(BF16)
    b_all = _dot(tril, hi) + _dot(tril, mid) + _dot(tril, lo)
    b_t = b_all.T
    g_t = gates.T

    heads = range(H)
    q = [q_ref[:, h * DK:(h + 1) * DK] for h in heads]
    k = [k_ref[:, h * DK:(h + 1) * DK] for h in heads]
    v = [v_ref[:, h * DV:(h + 1) * DV] for h in heads]
    bc = [b_all[:, H + h:H + h + 1] for h in heads]
    igc = [gates[:, h:h + 1] for h in heads]
    m = [m_scr[h][0:1, 0:1] for h in heads]
    c_old = [c_scr[h] for h in heads]
    n_old = [n_scr[h][0:1, :] for h in heads]

    qk = [_dot_nt(q[h], k[h]) for h in heads]
    inter = [_dot(q[h], c_old[h].astype(BF16)) for h in heads]

    upd, kw_sum, decay, m_new = [], [], [], []
    for h in heads:
        b_last = bc[h][L - 1:L, :]
        d_state = b_last - bc[h] + igc[h]
        m_new.append(jnp.maximum(b_last + m[h], jnp.max(d_state, axis=0, keepdims=True)))
        w_s = jnp.exp(d_state - m_new[h])
        decay.append(jnp.exp(b_last + m[h] - m_new[h]))
        kw = k[h].astype(F32) * w_s
        kw_sum.append(jnp.sum(kw, axis=0, keepdims=True))
        upd.append(_dot(kw.T.astype(BF16), v[h]))

    d_log, gm, m_t, qn = [], [], [], []
    for h in heads:
        c_scr[h] = decay[h] * c_old[h] + upd[h]
        n_scr[h] = jnp.broadcast_to(decay[h] * n_old[h] + kw_sum[h], (8, DK))
        m_scr[h] = jnp.broadcast_to(m_new[h], (8, LANES))
        br = b_t[H + h:H + h + 1, :]
        igr = g_t[h:h + 1, :]
        d_log.append(jnp.where(causal, bc[h] - br + igr, NEG))
        gm.append(bc[h] + m[h])
        m_t.append(jnp.maximum(gm[h], jnp.max(d_log[h], axis=1, keepdims=True)))
        qn.append(jnp.sum(q[h].astype(F32) * n_old[h], axis=1, keepdims=True))
    s, intra = [], []
    for h in heads:
        s.append(qk[h] * jnp.exp(d_log[h] - m_t[h]))
        intra.append(_dot(s[h].astype(BF16), v[h]))
    for h in heads:
        w_inter = jnp.exp(gm[h] - m_t[h])
        num = w_inter * inter[h] + intra[h]
        den = w_inter * qn[h] + jnp.sum(s[h], axis=1, keepdims=True)
        hout = num / jnp.maximum(jnp.abs(den), jnp.exp(-m_t[h]))
        sl = slice(h * DV, (h + 1) * DV)
        a_ref[:, sl] = (og_ref[:, sl].astype(F32) * hout).astype(BF16)


def _mlstm_chunk(u, gates, b_gates, batch, seq):
    t = u.shape[0]
    L = L_MLSTM
    nc = seq // L
    qk = MLSTM_HEADS * MLSTM_DK
    row = lambda col: (lambda b, c: (b * nc + c, col))
    return pl.pallas_call(
        _mlstm_chunk_body,
        grid=(batch, nc),
        in_specs=[pl.BlockSpec((L, qk), row(0)), pl.BlockSpec((L, qk), row(1)),
                  pl.BlockSpec((L, MLSTM_INNER), row(1)), pl.BlockSpec((L, MLSTM_INNER), row(2)),
                  pl.BlockSpec((L, LANES), row(0)),
                  pl.BlockSpec((1, LANES), lambda b, c: (0, 0))],
        out_specs=pl.BlockSpec((L, MLSTM_INNER), row(0)),
        out_shape=jax.ShapeDtypeStruct((t, MLSTM_INNER), BF16),
        scratch_shapes=[pltpu.VMEM((MLSTM_HEADS, MLSTM_DK, MLSTM_DV), F32),
                        pltpu.VMEM((MLSTM_HEADS, 8, MLSTM_DK), F32),
                        pltpu.VMEM((MLSTM_HEADS, 8, LANES), F32)],
        compiler_params=_params("parallel", "arbitrary"),
        name="mlstm_chunk",
    )(u, u, u, u, gates, b_gates)


def _mla_weights(w_in, w_q_b, w_kv_b):
    half = MLA_ROPE // 2
    c_q = w_in[:, :MLA_Q_LORA]
    c_kv = w_in[:, MLA_Q_LORA:MLA_Q_LORA + MLA_KV_LORA]
    r0 = MLA_Q_LORA + MLA_KV_LORA
    k1 = w_in[:, r0:r0 + half]
    k2 = w_in[:, r0 + half:r0 + MLA_ROPE]
    z = w_in[:, r0 + MLA_ROPE:]
    pad = jnp.zeros((D_MODEL, LANES - 4 * half), w_in.dtype)
    win = jnp.concatenate([c_kv, c_q, pad, k1, k2, k2, k1, z], axis=1).astype(BF16)

    wq = w_q_b.reshape(MLA_Q_LORA, MLA_HEADS, MLA_NOPE + MLA_ROPE)
    r1 = wq[..., MLA_NOPE:MLA_NOPE + half]
    r2 = wq[..., MLA_NOPE + half:]
    nope, rot = wq[..., :MLA_NOPE], jnp.concatenate([r1, r2, r2, r1], axis=-1)
    even = jnp.concatenate([nope, rot], axis=-1)[:, 0::2]
    odd = jnp.concatenate([rot, nope], axis=-1)[:, 1::2]
    wq = jnp.stack([even, odd], axis=2).reshape(MLA_Q_LORA, MLA_HEADS * HEAD_PAD).astype(BF16)

    wkv = w_kv_b.reshape(MLA_KV_LORA, MLA_HEADS, MLA_NOPE + MLA_V)
    wk = wkv[..., :MLA_NOPE].reshape(MLA_KV_LORA, MLA_HEADS * MLA_NOPE).astype(BF16)
    wvt =wkv[..., MLA_NOPE:].reshape(MLA_KV_LORA, MLA_HEADS * MLA_V).T.astype(BF16)
    return win, wq, wk, wvt


def _rope_tables(positions):
    half = MLA_ROPE // 2
    inv = ROPE_THETA ** (-jnp.arange(0, MLA_ROPE, 2, dtype=F32) / MLA_ROPE)
    inv_lane = jnp.concatenate([jnp.zeros((MLA_NOPE,), F32), jnp.tile(inv, 4)])
    ang = positions.astype(F32).reshape(-1)[:, None] * inv_lane[None, :]
    grp = ((jnp.arange(LANES) - MLA_NOPE) // half)[None, :]
    sin = jnp.sin(ang)
    rot = jnp.where(grp < 2, jnp.cos(ang), jnp.where(grp == 2, -sin, sin))
    scale = (MLA_NOPE + MLA_ROPE) ** -0.5 * LOG2E
    tq = jnp.where(grp < 0, 1.0, rot) * scale
    tk = jnp.where(grp < 0, 0.0, rot)
    return tq, tk


def kernel(x, p, positions, norm_g, mla_w_in, mla_q_norm, mla_w_q_b, mla_kv_norm, mla_w_kv_b, mla_w_out,
           conv_w_in, conv_w, conv_w_out, mlstm_w_in, mlstm_b_gates, mlstm_w_out, ple_proj, ple_gate,
           final_norm):
    batch, seq, d = x.shape
    t = batch * seq
    xs = x.reshape(t, d)
    p_all = p.reshape(DEPTH, t, PLE_DIM)
    tq_tab, tk_tab = _rope_tables(positions)
    fn =final_norm.reshape(1, d)
    for i in range(DEPTH):
        kind, j = i % 3, i // 3
        g = norm_g[i].reshape(1, d)
        if kind == 0:
            win, wq, wk, wvt = _mla_weights(mla_w_in[j], mla_w_q_b[j], mla_w_kv_b[j])
            q, k, vt, zs = _mla_proj(xs, g, win, mla_q_norm[j].reshape(1, -1), wq,
                                     mla_kv_norm[j].reshape(1, -1), wk, wvt, tq_tab, tk_tab)
            a = _mla_attn(q, k, vt, zs, batch, seq)
            w_out = mla_w_out[j]
        elif kind == 1:
            a = _conv_mix(xs, g, conv_w_in[j].astype(BF16), conv_w[j], batch, seq)
            w_out = conv_w_out[j]
        else:
            w = mlstm_w_in[j]
            n_main = w.shape[1] - 2 * MLSTM_HEADS
            wg = jnp.pad(w[:, n_main:], ((0, 0), (0, LANES - 2 * MLSTM_HEADS))).astype(BF16)
            bg = jnp.pad(mlstm_b_gates[j], (0, LANES - 2 * MLSTM_HEADS)).reshape(1, LANES)
            u, gates = _mlstm_proj(xs, g, w.astype(BF16), wg)
            a = _mlstm_chunk(u, gates, bg, batch, seq)
            w_out = mlstm_w_out[j]
        xs = _out_ple(a, w_out.astype(BF16), xs, p_all, i, ple_proj[i].astype(BF16),
                      ple_gate[i].astype(BF16), fn, final=(i == DEPTH - 1))
    return xs.reshape(batch, seq, d)
```

```python
import functools

import jax
import jax.numpy as jnp
from jax import lax
from jax.experimental import pallas as pl
from jax.experimental.pallas import tpu as pltpu

F32 = jnp.float32
BF16 = jnp.bfloat16

D_MODEL = 1024
DEPTH = 4
CHUNK = 64
PLE_DIM = 256
EPS = 1e-6
MLA_HEADS = 16
MLA_Q_LORA = 384
MLA_KV_LORA = 256
MLA_NOPE = 64
MLA_ROPE = 32
MLA_V = 64
ROPE_THETA = 10000.0
CONV_DIM = D_MODEL
MLSTM_HEADS = 4
MLSTM_INNER = 2 * D_MODEL
MLSTM_DV = MLSTM_INNER // MLSTM_HEADS
MLSTM_DK = MLSTM_DV // 2
MLSTM_U = 2 * MLSTM_HEADS * MLSTM_DK + 2 * MLSTM_INNER

LANES = 128
MXU_N = 256
HEAD_PAD = 128
NEG = -1e30
LOG2E = 1.4426950408889634
VMEM_LIMIT = 48 * 1024 * 1024

TM_OUT = 1024
TM_MLA = 512
MLA_ROWS = 256
TQ = 256
ATTN_HEADS = 4
ATTN_LAG = 5
TM_CONV = 512
CONV_ROWS = 256
TM_MLSTM = 512
MLSTM_ROWS = 256
TN_MLSTM = 1024
L_MLSTM = 256


def _rms(x):
    return x * lax.rsqrt(jnp.mean(x * x, axis=-1, keepdims=True) + EPS)


def _dot(a, b):
    return jnp.dot(a, b, preferred_element_type=F32)


def _dot_nt(a, b):
    return lax.dot_general(a, b, (((1,), (1,)), ((), ())), preferred_element_type=F32)


def _silu(z):
    return z * jax.nn.sigmoid(z)


def _params(*sem, flags=None):
    return pltpu.CompilerParams(dimension_semantics=sem, vmem_limit_bytes=VMEM_LIMIT, flags=flags)


def _out_ple_body(a_ref, w_ref, x_ref, p_ref, pp_ref, pg_ref, fn_ref, o_ref, *, final):
    x1 = x_ref[...] + _dot(a_ref[...], w_ref[...])
    gate = jax.nn.sigmoid(_dot(_rms(x1).astype(BF16), pg_ref[...]))
    x2 = x1 + gate * _dot(p_ref[...].astype(BF16), pp_ref[...])
    if final:
        x2 = _rms(x2) * fn_ref[...]
    o_ref[...] = x2


def _out_ple(a, w_out, x, p_all, layer, ple_proj, ple_gate, final_norm, final):
    t, k = a.shape
    tm = TM_OUT
    const = lambda i: (0, 0)
    return pl.pallas_call(
        functools.partial(_out_ple_body, final=final),
        grid=(t // tm,),
        in_specs=[
            pl.BlockSpec((tm, k), lambda i: (i, 0)),
            pl.BlockSpec((k, D_MODEL), const),
            pl.BlockSpec((tm, D_MODEL), lambda i: (i, 0)),
            pl.BlockSpec((None, tm, PLE_DIM), lambda i: (layer, i, 0)),
            pl.BlockSpec((None, PLE_DIM, D_MODEL), lambda i: (layer, 0, 0)),
            pl.BlockSpec((None, D_MODEL, D_MODEL), lambda i: (layer, 0, 0)),
            pl.BlockSpec((1, D_MODEL), const),
        ],
        out_specs=pl.BlockSpec((tm, D_MODEL), lambda i: (i, 0)),
        out_shape=jax.ShapeDtypeStruct((t, D_MODEL), F32),
        compiler_params=_params("parallel"),
        name="out_ple",
    )(a, w_out, x, p_all, ple_proj, ple_gate, final_norm)


def _mla_proj_body(x_ref, g_ref, win_ref, qn_ref, wq_ref, kvn_ref, wk_ref, wvt_ref,
                   tq_ref, tk_ref, q_ref, k_ref, vt_ref, zs_ref):
    z0 = MLA_KV_LORA + 512
    groups = [slice(r, r + MLA_ROWS) for r in range(0, x_ref.shape[0], MLA_ROWS)]
    st = [dict() for _ in groups]

    def stage_norm(n):
        rows = groups[n]
        st[n]["h"] = (_rms(x_ref[rows, :]) * g_ref[...]).astype(BF16)

    def stage_latents(n):
        rows, h = groups[n], st[n]["h"]
        ckv = _dot(h, win_ref[:, 0:MLA_KV_LORA])
        u2 = _dot(h, win_ref[:, MLA_KV_LORA:MLA_KV_LORA + 512])
        cq = u2[:, 0:MLA_Q_LORA]
        kr = u2[:, MLA_Q_LORA:MLA_Q_LORA + LANES]
        st[n]["cqn"] = (_rms(cq) * qn_ref[...]).astype(BF16)
        st[n]["ckvn"] = (_rms(ckv) * kvn_ref[...]).astype(BF16)
        lane = lax.broadcasted_iota(jnp.int32, kr.shape, 1)
        quarter = MLA_ROPE
        g4 = kr * tk_ref[rows, :]
        kr_hi = g4 + jnp.where(lane < MLA_NOPE + quarter, pltpu.roll(g4, LANES - quarter, 1),
                               pltpu.roll(g4, quarter, 1))
        st[n]["kr_hi"] = kr_hi
        st[n]["kr_lo"] = pltpu.roll(kr_hi, MLA_NOPE, 1)
        st[n]["low"] = lane < MLA_NOPE

    def stage_gate(n):
        rows, h = groups[n], st[n]["h"]
        for j in range(MLA_HEADS * MLA_V // MXU_N):
            sl = slice(j * MXU_N, (j + 1) * MXU_N)
            z = _dot(h, win_ref[:, z0 + j * MXU_N:z0 + (j + 1) * MXU_N])
            zs_ref[rows, sl] = _silu(z).astype(BF16)

    def stage_heads(n):
        rows, g = groups[n], st[n]
        tq = tq_ref[rows, :]
        tq2 = jnp.concatenate([tq, pltpu.roll(tq, MLA_NOPE, 1)], axis=1)
        for j in range(MLA_HEADS * HEAD_PAD // MXU_N):
            sl = slice(j * MXU_N, (j + 1) * MXU_N)
            q_ref[rows, sl] = (_dot(g["cqn"], wq_ref[:, sl]) * tq2).astype(BF16)
        for c in range(MLA_HEADS * MLA_NOPE // MXU_N):
            kn = _dot(g["ckvn"], wk_ref[:, c * MXU_N:(c + 1) * MXU_N])
            for half in range(2):
                pair = kn[:, half * LANES:(half + 1) * LANES]
                col = (2 * c + half) * MXU_N
                k_ref[rows, col:col + LANES] = jnp.where(g["low"], pair, g["kr_hi"]).astype(BF16)
                k_ref[rows, col + LANES:col + MXU_N] = jnp.where(g["low"], g["kr_lo"], pair).astype(BF16)
        for j in range(MLA_HEADS * MLA_V // MXU_N):
            sl = slice(j * MXU_N, (j + 1) * MXU_N)
            vt_ref[sl, rows] = _dot_nt(wvt_ref[sl, :], g["ckvn"]).astype(BF16)
        st[n] = None

    stage_norm(0)
    stage_latents(0)
    for n in range(len(groups)):
        if n + 1 < len(groups):
            stage_norm(n + 1)
        stage_gate(n)
        if n + 1 < len(groups):
            stage_latents(n + 1)
        stage_heads(n)


def _mla_proj(x, g, win, qn, wq, kvn, wk, wvt, tq_tab, tk_tab):
    t = x.shape[0]
    tm = TM_MLA
    hw = MLA_HEADS * HEAD_PAD
    vw = MLA_HEADS * MLA_V
    const = lambda i: (0, 0)
    row = lambda i: (i, 0)
    full = lambda arr: pl.BlockSpec(arr.shape, const)
    return pl.pallas_call(
        _mla_proj_body,
        grid=(t // tm,),
        in_specs=[pl.BlockSpec((tm, D_MODEL), row), full(g), full(win), full(qn), full(wq), full(kvn),
                  full(wk), full(wvt),
                  pl.BlockSpec((tm, LANES), row), pl.BlockSpec((tm, LANES), row)],
        out_specs=[pl.BlockSpec((tm, hw), row), pl.BlockSpec((tm, hw), row),
                   pl.BlockSpec((vw, tm), lambda i: (0, i)), pl.BlockSpec((tm, vw), row)],
        out_shape=[jax.ShapeDtypeStruct((t, hw), BF16), jax.ShapeDtypeStruct((t, hw), BF16),
                   jax.ShapeDtypeStruct((vw, t), BF16), jax.ShapeDtypeStruct((t, vw), BF16)],
        compiler_params=_params("parallel"),
        name="mla_proj",
    )(x, g, win, qn, wq, kvn, wk, wvt, tq_tab, tk_tab)


def _mla_attn_body(q_ref, k_ref, vt_ref, zs_ref, a_ref):
    seq = q_ref.shape[0]
    nq = seq // TQ
    key_c = lax.broadcasted_iota(jnp.int32, (TQ, TQ), 0) // CHUNK
    qry_c = lax.broadcasted_iota(jnp.int32, (TQ, TQ), 1) // CHUNK
    diag_ok = key_c <= qry_c
    units = [(i, hh, j) for i in range(nq) for hh in range(ATTN_HEADS) for j in range(i + 1)]
    lead = nq + 2
    lag = ATTN_LAG
    scores, probs = {}, {}
    m_run, l_run, acc = {}, {}, {}
    outs = {}

    def score_unit(u):
        i, hh, j = units[u]
        hs = slice(hh * HEAD_PAD, (hh + 1) * HEAD_PAD)
        s = _dot_nt(k_ref[j * TQ:(j + 1) * TQ, hs], q_ref[i * TQ:(i + 1) * TQ, hs])
        if j == i:
            s = jnp.where(diag_ok, s, NEG)
        mj = jnp.max(s, axis=0, keepdims=True)
        m_run[i, hh] = mj if j == 0 else jnp.maximum(m_run[i, hh], mj)
        scores[u] = s

    def exp_unit(u):
        i, hh, j = units[u]
        p = jnp.exp2(scores.pop(u) - m_run[i, hh])
        lj = jnp.sum(p, axis=0, keepdims=True)
        l_run[i, hh] = lj if j == 0 else l_run[i, hh] + lj
        probs[u] = p.astype(BF16)

    def value_unit(u):
        i, hh, j = units[u]
        oj = _dot(vt_ref[hh * MLA_V:(hh + 1) * MLA_V, j * TQ:(j + 1) * TQ], probs.pop(u))
        acc[i, hh] = oj if j == 0 else acc[i, hh] + oj

    def finish_unit(u):
        i, hh, j = units[u]
        if j != i:
            return
        outs[hh] = acc.pop((i, hh)) * (1.0 / l_run.pop((i, hh)))
        if hh % 2 == 1:
            rows = slice(i * TQ, (i + 1) * TQ)
            cols = slice((hh - 1) * MLA_V, (hh + 1) * MLA_V)
            o = jnp.concatenate([outs.pop(hh - 1), outs.pop(hh)], axis=0).T
            a_ref[rows, cols] = (o * zs_ref[rows, cols].astype(F32)).astype(BF16)

    n_units = len(units)
    for t in range(-lead, n_units + lag + 3):
        if 0 <= t + lead < n_units:
            score_unit(t + lead)
        if 0 <= t - lag < n_units:
            value_unit(t - lag)
        if 0 <= t < n_units:
            exp_unit(t)
        if 0 <= t - lag - 2 < n_units:
            finish_unit(t - lag - 2)


def _mla_attn(q, k, vt, zs, batch, seq):
    t = q.shape[0]
    nh = ATTN_HEADS
    blk = lambda w: pl.BlockSpec((seq, w), lambda b, h: (b, h))
    return pl.pallas_call(
        _mla_attn_body,
        grid=(batch, MLA_HEADS // nh),
        in_specs=[blk(nh * HEAD_PAD), blk(nh * HEAD_PAD),
                  pl.BlockSpec((nh * MLA_V, seq), lambda b, h: (h, b)), blk(nh * MLA_V)],
        out_specs=blk(nh * MLA_V),
        out_shape=jax.ShapeDtypeStruct((t, MLA_HEADS * MLA_V), BF16),
        compiler_params=_params("parallel", "parallel"),
        name="mla_attn",
    )(q, k, vt, zs)


def _conv_body(x_ref, g_ref, win_ref, cw_ref, a_ref, ext_ref):
    tm = TM_CONV
    t = pl.program_id(1)

    @pl.when(t == 0)
    def _():
        ext_ref[0:8, :] = jnp.zeros((8, CONV_DIM), F32)

    @pl.when(t > 0)
    def _():
        ext_ref[0:8, :] = ext_ref[tm:tm + 8, :]

    norm = lambda r: (_rms(x_ref[r:r + CONV_ROWS, :]) * g_ref[...]).astype(BF16)
    h_next = norm(0)
    for r in range(0, tm, CONV_ROWS):
        h = h_next
        for c in range(CONV_DIM // MXU_N):
            sl = slice(c * MXU_N, (c + 1) * MXU_N)
            col = lambda part: slice(part * CONV_DIM + c * MXU_N, part * CONV_DIM + (c + 1) * MXU_N)
            prod = _dot(h, win_ref[:, col(1)]) * _dot(h, win_ref[:, col(2)])
            ext_ref[8 + r:8 + r + CONV_ROWS, sl] = prod
            y = (cw_ref[2:3, sl] * prod + cw_ref[1:2, sl] * ext_ref[7 + r:7 + r + CONV_ROWS, sl]
                 + cw_ref[0:1, sl] * ext_ref[6 + r:6 + r + CONV_ROWS, sl])
            bg = _dot(h, win_ref[:, col(0)])
            z = _dot(h, win_ref[:, col(3)])
            a_ref[r:r + CONV_ROWS, sl] = (bg * y * _silu(z)).astype(BF16)
            if c == 0 and r + CONV_ROWS < tm:
                h_next = norm(r + CONV_ROWS)


def _conv_mix(x, g, win, cw, batch, seq):
    t = x.shape[0]
    tm = TM_CONV
    nt = seq // tm
    const = lambda b, i: (0, 0)
    row = lambda b, i: (b * nt + i, 0)
    return pl.pallas_call(
        _conv_body,
        grid=(batch, nt),
        in_specs=[pl.BlockSpec((tm, D_MODEL), row), pl.BlockSpec(g.shape, const),
                  pl.BlockSpec(win.shape, const), pl.BlockSpec(cw.shape, const)],
        out_specs=pl.BlockSpec((tm, CONV_DIM), row),
        out_shape=jax.ShapeDtypeStruct((t, CONV_DIM), BF16),
        scratch_shapes=[pltpu.VMEM((tm + 8, CONV_DIM), F32)],
        compiler_params=_params("parallel", "arbitrary"),
        name="conv_mix",
    )(x, g, win, cw)


def _mlstm_proj_body(x_ref, g_ref, w_ref, wg_ref, u_ref, gates_ref):
    tn = TN_MLSTM
    qk = MLSTM_HEADS * MLSTM_DK
    cols = lambda start, j: slice(start + j * tn, start + (j + 1) * tn)
    norm = lambda r: (_rms(x_ref[r:r + MLSTM_ROWS, :]) * g_ref[...]).astype(BF16)
    h_next = norm(0)
    for r in range(0, x_ref.shape[0], MLSTM_ROWS):
        h = h_next
        rows = slice(r, r + MLSTM_ROWS)
        proj = lambda sl: _dot(h, w_ref[:, sl])
        gates_ref[rows, :] = _dot(h, wg_ref[...])
        for j in range(qk // tn):
            u_ref[rows, cols(0, j)] = proj(cols(0, j)).astype(BF16)
            u_ref[rows, cols(qk, j)] = (proj(cols(qk, j)) * (MLSTM_DK ** -0.5)).astype(BF16)
        if r + MLSTM_ROWS < x_ref.shape[0]:
            h_next = norm(r + MLSTM_ROWS)
        for j in range(MLSTM_INNER // tn):
            u_ref[rows, cols(2 * qk, j)] = proj(cols(2 * qk, j)).astype(BF16)
            o = proj(cols(2 * qk + MLSTM_INNER, j))
            z = proj(cols(2 * qk + 2 * MLSTM_INNER, j))
            u_ref[rows, cols(2 * qk + MLSTM_INNER, j)] = (jax.nn.sigmoid(o) * _silu(z)).astype(BF16)


def _mlstm_proj(x, g, w, wg):
    t = x.shape[0]
    n = w.shape[1] - 2 * MLSTM_HEADS
    tm = TM_MLSTM
    const = lambda i: (0, 0)
    row = lambda i: (i, 0)
    return pl.pallas_call(
        _mlstm_proj_body,
        grid=(t // tm,),
        in_specs=[pl.BlockSpec((tm, D_MODEL), row),
                  pl.BlockSpec(g.shape, const),
                  pl.BlockSpec((D_MODEL, n), const, pipeline_mode=pl.Buffered(1)),
                  pl.BlockSpec(wg.shape, const)],
        out_specs=[pl.BlockSpec((tm, MLSTM_U), row), pl.BlockSpec((tm, LANES), row)],
        out_shape=[jax.ShapeDtypeStruct((t, MLSTM_U), BF16), jax.ShapeDtypeStruct((t, LANES), F32)],
        compiler_params=_params("parallel"),
        name="mlstm_proj",
    )(x, g, w, wg)


def _mlstm_chunk_body(q_ref, k_ref, v_ref, og_ref, g_ref, bg_ref, a_ref, c_scr, m_scr):
    L = L_MLSTM
    H, DK, DV = MLSTM_HEADS, MLSTM_DK, MLSTM_DV

    @pl.when(pl.program_id(1) == 0)
    def _():
        c_scr[...] = jnp.zeros_like(c_scr)
        m_scr[...] = jnp.zeros_like(m_scr)

    gates = g_ref[...] + bg_ref[...]
    lf = jnp.minimum(gates, 0.0) - jnp.log(1.0 + jnp.exp(-jnp.abs(gates)))
    r_i = lax.broadcasted_iota(jnp.int32, (L, L), 0)
    c_i = lax.broadcasted_iota(jnp.int32, (L, L), 1)
    causal = c_i <= r_i
    tril = causal.astype(BF16)
    hi = lf.astype(BF16)
    r1 = lf - hi.astype(F32)
    mid = r1.astype(BF16)
    lo = (r1 - mid.astype(F32)).astype(BF16)
    b_all = _dot(tril, hi) + _dot(tril, mid) + _dot(tril, lo)
    b_t = b_all.T
    g_t = gates.T

    heads = range(H)
    ones = jnp.ones((L, LANES), BF16)
    q = [q_ref[:, h * DK:(h + 1) * DK] for h in heads]
    k = [k_ref[:, h * DK:(h + 1) * DK] for h in heads]
    v = [jnp.concatenate([v_ref[:, h * DV:(h + 1) * DV], ones], axis=1) for h in heads]
    bc = [b_all[:, H + h:H + h + 1] for h in heads]
    igc = [gates[:, h:h + 1] for h in heads]
    m = [m_scr[h][0:1, 0:1] for h in heads]
    c_old = [c_scr[h] for h in heads]

    qk = [_dot_nt(q[h], k[h]) for h in heads]

    upd, decay, m_new = [], [], []
    for h in heads:
        b_last = bc[h][L - 1:L, :]
        d_state = b_last - bc[h] + igc[h]
        m_new.append(jnp.maximum(b_last + m[h], jnp.max(d_state, axis=0, keepdims=True)))
        w_s = jnp.exp(d_state - m_new[h])
        decay.append(jnp.exp(b_last + m[h] - m_new[h]))
        kw = k[h].astype(F32) * w_s
        upd.append(_dot(kw.T.astype(BF16), v[h]))

    d_log, gm, m_t = [], [], []
    for h in heads:
        c_scr[h] = decay[h] * c_old[h] + upd[h]
        m_scr[h] = jnp.broadcast_to(m_new[h], (8, LANES))
        br = b_t[H + h:H + h + 1, :]
        igr = g_t[h:h + 1, :]
        d_log.append(jnp.where(causal, bc[h] - br + igr, NEG))
        gm.append(bc[h] + m[h])
        m_t.append(jnp.maximum(gm[h], jnp.max(d_log[h], axis=1, keepdims=True)))
    num = []
    for h in heads:
        qs = (q[h].astype(F32) * jnp.exp(gm[h] - m_t[h])).astype(BF16)
        s = (qk[h] * jnp.exp(d_log[h] - m_t[h])).astype(BF16)
        lhs = jnp.concatenate([qs, s], axis=1)
        rhs = jnp.concatenate([c_old[h].astype(BF16), v[h]], axis=0)
        num.append(_dot(lhs, rhs))
    for h in heads:
        den = num[h][:, DV:DV + 1]
        hout = num[h][:, :DV] / jnp.maximum(jnp.abs(den), jnp.exp(-m_t[h]))
        sl = slice(h * DV, (h + 1) * DV)
        a_ref[:, sl] = (og_ref[:, sl].astype(F32) * hout).astype(BF16)


def _mlstm_chunk(u, gates, b_gates, batch, seq):
    t = u.shape[0]
    L = L_MLSTM
    nc = seq // L
    qk = MLSTM_HEADS * MLSTM_DK
    row = lambda col: (lambda b, c: (b * nc + c, col))
    return pl.pallas_call(
        _mlstm_chunk_body,
        grid=(batch, nc),
        in_specs=[pl.BlockSpec((L, qk), row(0)), pl.BlockSpec((L, qk), row(1)),
                  pl.BlockSpec((L, MLSTM_INNER), row(1)), pl.BlockSpec((L, MLSTM_INNER), row(2)),
                  pl.BlockSpec((L, LANES), row(0)),
                  pl.BlockSpec((1, LANES), lambda b, c: (0, 0))],
        out_specs=pl.BlockSpec((L, MLSTM_INNER), row(0)),
        out_shape=jax.ShapeDtypeStruct((t, MLSTM_INNER), BF16),
        scratch_shapes=[pltpu.VMEM((MLSTM_HEADS, MLSTM_DK, MLSTM_DV + LANES), F32),
                        pltpu.VMEM((MLSTM_HEADS, 8, LANES), F32)],
        compiler_params=_params("parallel", "arbitrary"),
        name="mlstm_chunk",
    )(u, u, u, u, gates, b_gates)


def _mla_weights(w_in, w_q_b, w_kv_b):
    half = MLA_ROPE // 2
    c_q = w_in[:, :MLA_Q_LORA]
    c_kv = w_in[:, MLA_Q_LORA:MLA_Q_LORA + MLA_KV_LORA]
    r0 = MLA_Q_LORA + MLA_KV_LORA
    k1 = w_in[:, r0:r0 + half]
    k2 = w_in[:, r0 + half:r0 + MLA_ROPE]
    z = w_in[:, r0 + MLA_ROPE:]
    pad = jnp.zeros((D_MODEL, LANES - 4 * half), w_in.dtype)
    win = jnp.concatenate([c_kv, c_q, pad, k1, k2, k2, k1, z], axis=1).astype(BF16)

    wq = w_q_b.reshape(MLA_Q_LORA, MLA_HEADS, MLA_NOPE + MLA_ROPE)
    r1 = wq[..., MLA_NOPE:MLA_NOPE + half]
    r2 = wq[..., MLA_NOPE + half:]
    nope, rot = wq[..., :MLA_NOPE], jnp.concatenate([r1, r2, r2, r1], axis=-1)
    even = jnp.concatenate([nope, rot], axis=-1)[:, 0::2]
    odd = jnp.concatenate([rot, nope], axis=-1)[:, 1::2]
    wq = jnp.stack([even, odd], axis=2).reshape(MLA_Q_LORA, MLA_HEADS * HEAD_PAD).astype(BF16)

    wkv = w_kv_b.reshape(MLA_KV_LORA, MLA_HEADS, MLA_NOPE + MLA_V)
    wk = wkv[..., :MLA_NOPE].reshape(MLA_KV_LORA, MLA_HEADS * MLA_NOPE).astype(BF16)
    wvt =wkv[..., MLA_NOPE:].reshape(MLA_KV_LORA, MLA_HEADS * MLA_V).T.astype(BF16)
    return win, wq, wk, wvt


def _rope_tables(positions):
    half = MLA_ROPE // 2
    inv = ROPE_THETA ** (-jnp.arange(0, MLA_ROPE, 2, dtype=F32) / MLA_ROPE)
    inv_lane = jnp.concatenate([jnp.zeros((MLA_NOPE,), F32), jnp.tile(inv, 4)])
    ang = positions.astype(F32).reshape(-1)[:, None] * inv_lane[None, :]
    grp = ((jnp.arange(LANES) - MLA_NOPE) // half)[None, :]
    sin = jnp.sin(ang)
    rot = jnp.where(grp < 2, jnp.cos(ang), jnp.where(grp == 2, -sin, sin))
    scale = (MLA_NOPE + MLA_ROPE) ** -0.5 * LOG2E
    tq = jnp.where(grp < 0, 1.0, rot) * scale
    tk = jnp.where(grp < 0, 0.0, rot)
    return tq, tk


def kernel(x, p, positions, norm_g, mla_w_in, mla_q_norm, mla_w_q_b, mla_kv_norm, mla_w_kv_b, mla_w_out,
           conv_w_in, conv_w, conv_w_out, mlstm_w_in, mlstm_b_gates, mlstm_w_out, ple_proj, ple_gate,
           final_norm):
    batch, seq, d = x.shape
    t = batch * seq
    xs = x.reshape(t, d)
    p_all = p.reshape(DEPTH, t, PLE_DIM)
    tq_tab, tk_tab = _rope_tables(positions)
    ple_proj_b, ple_gate_b = ple_proj.astype(BF16), ple_gate.astype(BF16)
    fn =final_norm.reshape(1, d)
    for i in range(DEPTH):
        kind, j = i % 3, i // 3
        g = norm_g[i].reshape(1, d)
        if kind == 0:
            win, wq, wk, wvt = _mla_weights(mla_w_in[j], mla_w_q_b[j], mla_w_kv_b[j])
            q, k, vt, zs = _mla_proj(xs, g, win, mla_q_norm[j].reshape(1, -1), wq,
                                     mla_kv_norm[j].reshape(1, -1), wk, wvt, tq_tab, tk_tab)
            a = _mla_attn(q, k, vt, zs, batch, seq)
            w_out = mla_w_out[j]
        elif kind == 1:
            a = _conv_mix(xs, g, conv_w_in[j].astype(BF16), conv_w[j], batch, seq)
            w_out = conv_w_out[j]
        else:
            w = mlstm_w_in[j]
            n_main = w.shape[1] - 2 * MLSTM_HEADS
            wg = jnp.pad(w[:, n_main:], ((0, 0), (0, LANES - 2 * MLSTM_HEADS))).astype(BF16)
            bg = jnp.pad(mlstm_b_gates[j], (0, LANES - 2 * MLSTM_HEADS)).reshape(1, LANES)
            u, gates = _mlstm_proj(xs, g, w.astype(BF16), wg)
            a = _mlstm_chunk(u, gates, bg, batch, seq)
            w_out = mlstm_w_out[j]
        xs = _out_ple(a, w_out.astype(BF16), xs, p_all, i, ple_proj_b, ple_gate_b, fn,
                      final=(i == DEPTH - 1))
    return xs.reshape(batch, seq, d)
```

```python
import functools

import jax
import jax.numpy as jnp
from jax import lax
from jax.experimental import pallas as pl
from jax.experimental.pallas import tpu as pltpu

F32 = jnp.float32
BF16 = jnp.bfloat16

D_MODEL = 1024
DEPTH = 4
CHUNK = 64
PLE_DIM = 256
EPS = 1e-6
MLA_HEADS = 16
MLA_Q_LORA = 384
MLA_KV_LORA = 256
MLA_NOPE = 64
MLA_ROPE = 32
MLA_V = 64
ROPE_THETA = 10000.0
CONV_DIM = D_MODEL
MLSTM_HEADS = 4
MLSTM_INNER = 2 * D_MODEL
MLSTM_DV = MLSTM_INNER // MLSTM_HEADS
MLSTM_DK = MLSTM_DV // 2
MLSTM_U = 2 * MLSTM_HEADS * MLSTM_DK + 2 * MLSTM_INNER

LANES = 128
MXU_N = 256
HEAD_PAD = 128
NEG = -1e30
LOG2E = 1.4426950408889634
VMEM_LIMIT = 48 * 1024 * 1024

TM_OUT = 1024
TM_MLA = 512
MLA_ROWS = 256
TQ = 256
ATTN_HEADS = 4
ATTN_LAG = 5
TM_CONV = 512
CONV_ROWS = 256
TM_MLSTM = 512
MLSTM_ROWS = 256
TN_MLSTM = 512
L_MLSTM = 256


def _rms(x):
    return x * lax.rsqrt(jnp.mean(x * x, axis=-1, keepdims=True) + EPS)


def _dot(a, b):
    return jnp.dot(a, b, preferred_element_type=F32)


def _dot_nt(a, b):
    return lax.dot_general(a, b, (((1,), (1,)), ((), ())), preferred_element_type=F32)


def _silu(z):
    return z * jax.nn.sigmoid(z)


def _params(*sem, flags=None):
    return pltpu.CompilerParams(dimension_semantics=sem, vmem_limit_bytes=VMEM_LIMIT, flags=flags)


def _out_ple_body(a_ref, w_ref, x_ref, p_ref, pp_ref, pg_ref, fn_ref, o_ref, *, final):
    x1 = x_ref[...] + _dot(a_ref[...], w_ref[...])
    gate = jax.nn.sigmoid(_dot(_rms(x1).astype(BF16), pg_ref[...]))
    x2 = x1 + gate * _dot(p_ref[...].astype(BF16), pp_ref[...])
    if final:
        x2 = _rms(x2) * fn_ref[...]
    o_ref[...] = x2


def _out_ple(a, w_out, x, p_all, layer, ple_proj, ple_gate, final_norm, final):
    t, k = a.shape
    tm = TM_OUT
    const = lambda i: (0, 0)
    return pl.pallas_call(
        functools.partial(_out_ple_body, final=final),
        grid=(t // tm,),
        in_specs=[
            pl.BlockSpec((tm, k), lambda i: (i, 0)),
            pl.BlockSpec((k, D_MODEL), const),
            pl.BlockSpec((tm, D_MODEL), lambda i: (i, 0)),
            pl.BlockSpec((None, tm, PLE_DIM), lambda i: (layer, i, 0)),
            pl.BlockSpec((PLE_DIM, D_MODEL), const),
            pl.BlockSpec((D_MODEL, D_MODEL), const),
            pl.BlockSpec((1, D_MODEL), const),
        ],
        out_specs=pl.BlockSpec((tm, D_MODEL), lambda i: (i, 0)),
        out_shape=jax.ShapeDtypeStruct((t, D_MODEL), F32),
        compiler_params=_params("parallel"),
        name="out_ple",
    )(a, w_out, x, p_all, ple_proj, ple_gate, final_norm)


def _mla_proj_body(x_ref, g_ref, win_ref, qn_ref, wq_ref, kvn_ref, wk_ref, wvt_ref,
                   tq_ref, tk_ref, q_ref, k_ref, vt_ref, zs_ref):
    z0 = MLA_KV_LORA + 512
    groups = [slice(r, r + MLA_ROWS) for r in range(0, x_ref.shape[0], MLA_ROWS)]
    st = [dict() for _ in groups]

    def stage_norm(n):
        rows = groups[n]
        st[n]["h"] = (_rms(x_ref[rows, :]) * g_ref[...]).astype(BF16)

    def stage_latents(n):
        rows, h = groups[n], st[n]["h"]
        ckv = _dot(h, win_ref[:, 0:MLA_KV_LORA])
        u2 = _dot(h, win_ref[:, MLA_KV_LORA:MLA_KV_LORA + 512])
        cq = u2[:, 0:MLA_Q_LORA]
        kr = u2[:, MLA_Q_LORA:MLA_Q_LORA + LANES]
        st[n]["cqn"] = (_rms(cq) * qn_ref[...]).astype(BF16)
        st[n]["ckvn"] = (_rms(ckv) * kvn_ref[...]).astype(BF16)
        lane = lax.broadcasted_iota(jnp.int32, kr.shape, 1)
        quarter = MLA_ROPE
        g4 = kr * tk_ref[rows, :]
        kr_hi = g4 + jnp.where(lane < MLA_NOPE + quarter, pltpu.roll(g4, LANES - quarter, 1),
                               pltpu.roll(g4, quarter, 1))
        st[n]["kr_hi"] = kr_hi
        st[n]["kr_lo"] = pltpu.roll(kr_hi, MLA_NOPE, 1)
        st[n]["low"] = lane < MLA_NOPE

    def stage_gate(n):
        rows, h = groups[n], st[n]["h"]
        for j in range(MLA_HEADS * MLA_V // MXU_N):
            sl = slice(j * MXU_N, (j + 1) * MXU_N)
            z = _dot(h, win_ref[:, z0 + j * MXU_N:z0 + (j + 1) * MXU_N])
            zs_ref[rows, sl] = _silu(z).astype(BF16)

    def stage_heads(n):
        rows, g = groups[n], st[n]
        tq = tq_ref[rows, :]
        tq2 = jnp.concatenate([tq, pltpu.roll(tq, MLA_NOPE, 1)], axis=1)
        for j in range(MLA_HEADS * HEAD_PAD // MXU_N):
            sl = slice(j * MXU_N, (j + 1) * MXU_N)
            q_ref[rows, sl] = (_dot(g["cqn"], wq_ref[:, sl]) * tq2).astype(BF16)
        for c in range(MLA_HEADS * MLA_NOPE // MXU_N):
            kn = _dot(g["ckvn"], wk_ref[:, c * MXU_N:(c + 1) * MXU_N])
            for half in range(2):
                pair = kn[:, half * LANES:(half + 1) * LANES]
                col = (2 * c + half) * MXU_N
                k_ref[rows, col:col + LANES] = jnp.where(g["low"], pair, g["kr_hi"]).astype(BF16)
                k_ref[rows, col + LANES:col + MXU_N] = jnp.where(g["low"], g["kr_lo"], pair).astype(BF16)
        for j in range(MLA_HEADS * MLA_V // MXU_N):
            sl = slice(j * MXU_N, (j + 1) * MXU_N)
            vt_ref[sl, rows] = _dot_nt(wvt_ref[sl, :], g["ckvn"]).astype(BF16)
        st[n] = None

    stage_norm(0)
    stage_latents(0)
    for n in range(len(groups)):
        if n + 1 < len(groups):
            stage_norm(n + 1)
        stage_gate(n)
        if n + 1 < len(groups):
            stage_latents(n + 1)
        stage_heads(n)


def _mla_proj(x, g, win, qn, wq, kvn, wk, wvt, tq_tab, tk_tab):
    t = x.shape[0]
    tm = TM_MLA
    hw = MLA_HEADS * HEAD_PAD
    vw = MLA_HEADS * MLA_V
    const = lambda i: (0, 0)
    row = lambda i: (i, 0)
    full = lambda arr: pl.BlockSpec(arr.shape, const)
    return pl.pallas_call(
        _mla_proj_body,
        grid=(t // tm,),
        in_specs=[pl.BlockSpec((tm, D_MODEL), row), full(g), full(win), full(qn), full(wq), full(kvn),
                  full(wk), full(wvt),
                  pl.BlockSpec((tm, LANES), row), pl.BlockSpec((tm, LANES), row)],
        out_specs=[pl.BlockSpec((tm, hw), row), pl.BlockSpec((tm, hw), row),
                   pl.BlockSpec((vw, tm), lambda i: (0, i)), pl.BlockSpec((tm, vw), row)],
        out_shape=[jax.ShapeDtypeStruct((t, hw), BF16), jax.ShapeDtypeStruct((t, hw), BF16),
                   jax.ShapeDtypeStruct((vw, t), BF16), jax.ShapeDtypeStruct((t, vw), BF16)],
        compiler_params=_params("parallel"),
        name="mla_proj",
    )(x, g, win, qn, wq, kvn, wk, wvt, tq_tab, tk_tab)


def _mla_attn_body(q_ref, k_ref, vt_ref, zs_ref, a_ref):
    seq = q_ref.shape[0]
    nq = seq // TQ
    key_c = lax.broadcasted_iota(jnp.int32, (TQ, TQ), 0) // CHUNK
    qry_c = lax.broadcasted_iota(jnp.int32, (TQ, TQ), 1) // CHUNK
    diag_ok = key_c <= qry_c
    units = [(i, hh, j) for i in range(nq) for hh in range(ATTN_HEADS) for j in range(i + 1)]
    lead = nq + 2
    lag = ATTN_LAG
    scores, probs = {}, {}
    m_run, l_run, acc = {}, {}, {}
    outs = {}

    def score_unit(u):
        i, hh, j = units[u]
        hs = slice(hh * HEAD_PAD, (hh + 1) * HEAD_PAD)
        s = _dot_nt(k_ref[j * TQ:(j + 1) * TQ, hs], q_ref[i * TQ:(i + 1) * TQ, hs])
        if j == i:
            s = jnp.where(diag_ok, s, NEG)
        mj = jnp.max(s, axis=0, keepdims=True)
        m_run[i, hh] = mj if j == 0 else jnp.maximum(m_run[i, hh], mj)
        scores[u] = s

    def exp_unit(u):
        i, hh, j = units[u]
        p = jnp.exp2(scores.pop(u) - m_run[i, hh])
        lj = jnp.sum(p, axis=0, keepdims=True)
        l_run[i, hh] = lj if j == 0 else l_run[i, hh] + lj
        probs[u] = p.astype(BF16)

    def value_unit(u):
        i, hh, j = units[u]
        oj = _dot(vt_ref[hh * MLA_V:(hh + 1) * MLA_V, j * TQ:(j + 1) * TQ], probs.pop(u))
        acc[i, hh] = oj if j == 0 else acc[i, hh] + oj

    def finish_unit(u):
        i, hh, j = units[u]
        if j != i:
            return
        outs[hh] = acc.pop((i, hh)) * (1.0 / l_run.pop((i, hh)))
        if hh % 2 == 1:
            rows = slice(i * TQ, (i + 1) * TQ)
            cols = slice((hh - 1) * MLA_V, (hh + 1) * MLA_V)
            o = jnp.concatenate([outs.pop(hh - 1), outs.pop(hh)], axis=0).T
            a_ref[rows, cols] = (o * zs_ref[rows, cols].astype(F32)).astype(BF16)

    n_units = len(units)
    for t in range(-lead, n_units + lag + 3):
        if 0 <= t + lead < n_units:
            score_unit(t + lead)
        if 0 <= t - lag < n_units:
            value_unit(t - lag)
        if 0 <= t < n_units:
            exp_unit(t)
        if 0 <= t - lag - 2 < n_units:
            finish_unit(t - lag - 2)


def _mla_attn(q, k, vt, zs, batch, seq):
    t = q.shape[0]
    nh = ATTN_HEADS
    blk = lambda w: pl.BlockSpec((seq, w), lambda b, h: (b, h))
    return pl.pallas_call(
        _mla_attn_body,
        grid=(batch, MLA_HEADS // nh),
        in_specs=[blk(nh * HEAD_PAD), blk(nh * HEAD_PAD),
                  pl.BlockSpec((nh * MLA_V, seq), lambda b, h: (h, b)), blk(nh * MLA_V)],
        out_specs=blk(nh * MLA_V),
        out_shape=jax.ShapeDtypeStruct((t, MLA_HEADS * MLA_V), BF16),
        compiler_params=_params("parallel", "parallel"),
        name="mla_attn",
    )(q, k, vt, zs)


def _conv_body(x_ref, g_ref, win_ref, cw_ref, a_ref, ext_ref):
    tm = TM_CONV
    t = pl.program_id(1)

    @pl.when(t == 0)
    def _():
        ext_ref[0:8, :] = jnp.zeros((8, CONV_DIM), F32)

    @pl.when(t > 0)
    def _():
        ext_ref[0:8, :] = ext_ref[tm:tm + 8, :]

    norm = lambda r: (_rms(x_ref[r:r + CONV_ROWS, :]) * g_ref[...]).astype(BF16)
    h_next = norm(0)
    for r in range(0, tm, CONV_ROWS):
        h = h_next
        for c in range(CONV_DIM // MXU_N):
            sl = slice(c * MXU_N, (c + 1) * MXU_N)
            col = lambda part: slice(part * CONV_DIM + c * MXU_N, part * CONV_DIM + (c + 1) * MXU_N)
            prod = _dot(h, win_ref[:, col(1)]) * _dot(h, win_ref[:, col(2)])
            ext_ref[8 + r:8 + r + CONV_ROWS, sl] = prod
            y = (cw_ref[2:3, sl] * prod + cw_ref[1:2, sl] * ext_ref[7 + r:7 + r + CONV_ROWS, sl]
                 + cw_ref[0:1, sl] * ext_ref[6 + r:6 + r + CONV_ROWS, sl])
            bg = _dot(h, win_ref[:, col(0)])
            z = _dot(h, win_ref[:, col(3)])
            a_ref[r:r + CONV_ROWS, sl] = (bg * y * _silu(z)).astype(BF16)
            if c == 0 and r + CONV_ROWS < tm:
                h_next = norm(r + CONV_ROWS)


def _conv_mix(x, g, win, cw, batch, seq):
    t = x.shape[0]
    tm = TM_CONV
    nt = seq // tm
    const = lambda b, i: (0, 0)
    row = lambda b, i: (b * nt + i, 0)
    return pl.pallas_call(
        _conv_body,
        grid=(batch, nt),
        in_specs=[pl.BlockSpec((tm, D_MODEL), row), pl.BlockSpec(g.shape, const),
                  pl.BlockSpec(win.shape, const), pl.BlockSpec(cw.shape, const)],
        out_specs=pl.BlockSpec((tm, CONV_DIM), row),
        out_shape=jax.ShapeDtypeStruct((t, CONV_DIM), BF16),
        scratch_shapes=[pltpu.VMEM((tm + 8, CONV_DIM), F32)],
        compiler_params=_params("parallel", "arbitrary"),
        name="conv_mix",
    )(x, g, win, cw)


def _mlstm_proj_body(x_ref, g_ref, w_ref, wg_ref, u_ref, gates_ref):
    tn = TN_MLSTM
    qk = MLSTM_HEADS * MLSTM_DK
    cols = lambda start, j: slice(start + j * tn, start + (j + 1) * tn)
    norm = lambda r: (_rms(x_ref[r:r + MLSTM_ROWS, :]) * g_ref[...]).astype(BF16)
    h_next = norm(0)
    for r in range(0, x_ref.shape[0], MLSTM_ROWS):
        h = h_next
        rows = slice(r, r + MLSTM_ROWS)
        proj = lambda sl: _dot(h, w_ref[:, sl])
        gates_ref[rows, :] = _dot(h, wg_ref[...])
        for j in range(qk // tn):
            u_ref[rows, cols(0, j)] = proj(cols(0, j)).astype(BF16)
            u_ref[rows, cols(qk, j)] = (proj(cols(qk, j)) * (MLSTM_DK ** -0.5)).astype(BF16)
        if r + MLSTM_ROWS < x_ref.shape[0]:
            h_next = norm(r + MLSTM_ROWS)
        for j in range(MLSTM_INNER // tn):
            u_ref[rows, cols(2 * qk, j)] = proj(cols(2 * qk, j)).astype(BF16)
            o = proj(cols(2 * qk + MLSTM_INNER, j))
            z = proj(cols(2 * qk + 2 * MLSTM_INNER, j))
            u_ref[rows, cols(2 * qk + MLSTM_INNER, j)] = (jax.nn.sigmoid(o) * _silu(z)).astype(BF16)


def _mlstm_proj(x, g, w, wg):
    t = x.shape[0]
    n = w.shape[1] - 2 * MLSTM_HEADS
    tm = TM_MLSTM
    const = lambda i: (0, 0)
    row = lambda i: (i, 0)
    return pl.pallas_call(
        _mlstm_proj_body,
        grid=(t // tm,),
        in_specs=[pl.BlockSpec((tm, D_MODEL), row),
                  pl.BlockSpec(g.shape, const),
                  pl.BlockSpec((D_MODEL, n), const, pipeline_mode=pl.Buffered(1)),
                  pl.BlockSpec(wg.shape, const)],
        out_specs=[pl.BlockSpec((tm, MLSTM_U), row), pl.BlockSpec((tm, LANES), row)],
        out_shape=[jax.ShapeDtypeStruct((t, MLSTM_U), BF16), jax.ShapeDtypeStruct((t, LANES), F32)],
        compiler_params=_params("parallel"),
        name="mlstm_proj",
    )(x, g, w, wg)


def _mlstm_chunk_body(q_ref, k_ref, v_ref, og_ref, g_ref, bg_ref, a_ref, c_scr, m_scr):
    L = L_MLSTM
    H, DK, DV = MLSTM_HEADS, MLSTM_DK, MLSTM_DV

    @pl.when(pl.program_id(1) == 0)
    def _():
        c_scr[...] = jnp.zeros_like(c_scr)
        m_scr[...] = jnp.zeros_like(m_scr)

    gates = g_ref[...] + bg_ref[...]
    lf = jnp.minimum(gates, 0.0) - jnp.log(1.0 + jnp.exp(-jnp.abs(gates)))
    r_i = lax.broadcasted_iota(jnp.int32, (L, L), 0)
    c_i = lax.broadcasted_iota(jnp.int32, (L, L), 1)
    causal = c_i <= r_i
    tril = causal.astype(BF16)
    hi = lf.astype(BF16)
    r1 = lf - hi.astype(F32)
    mid = r1.astype(BF16)
    lo = (r1 - mid.astype(F32)).astype(BF16)
    b_all = _dot(tril, hi) + _dot(tril, mid) + _dot(tril, lo)
    b_t = b_all.T
    g_t = gates.T

    heads = range(H)
    ones = jnp.ones((L, LANES), BF16)
    q = [q_ref[:, h * DK:(h + 1) * DK] for h in heads]
    k = [k_ref[:, h * DK:(h + 1) * DK] for h in heads]
    v = [jnp.concatenate([v_ref[:, h * DV:(h + 1) * DV], ones], axis=1) for h in heads]
    bc = [b_all[:, H + h:H + h + 1] for h in heads]
    igc = [gates[:, h:h + 1] for h in heads]
    m = [m_scr[h][0:1, 0:1] for h in heads]
    c_old = [c_scr[h] for h in heads]

    qk = [_dot_nt(q[h], k[h]) for h in heads]

    upd, decay, m_new = [], [], []
    for h in heads:
        b_last = bc[h][L - 1:L, :]
        d_state = b_last - bc[h] + igc[h]
        m_new.append(jnp.maximum(b_last + m[h], jnp.max(d_state, axis=0, keepdims=True)))
        w_s = jnp.exp(d_state - m_new[h])
        decay.append(jnp.exp(b_last + m[h] - m_new[h]))
        kw = k[h].astype(F32) * w_s
        upd.append(_dot(kw.T.astype(BF16), v[h]))

    d_log, gm, m_t = [], [], []
    for h in heads:
        c_scr[h] = decay[h] * c_old[h] + upd[h]
        m_scr[h] = jnp.broadcast_to(m_new[h], (8, LANES))
        br = b_t[H + h:H + h + 1, :]
        igr = g_t[h:h + 1, :]
        d_log.append(jnp.where(causal, bc[h] - br + igr, NEG))
        gm.append(bc[h] + m[h])
        m_t.append(jnp.maximum(gm[h], jnp.max(d_log[h], axis=1, keepdims=True)))
    num = []
    for h in heads:
        qs = (q[h].astype(F32) * jnp.exp(gm[h] - m_t[h])).astype(BF16)
        s = (qk[h] * jnp.exp(d_log[h] - m_t[h])).astype(BF16)
        lhs = jnp.concatenate([qs, s], axis=1)
        rhs = jnp.concatenate([c_old[h].astype(BF16), v[h]], axis=0)
        num.append(_dot(lhs, rhs))
    for h in heads:
        den = num[h][:, DV:DV + 1]
        hout = num[h][:, :DV] / jnp.maximum(jnp.abs(den), jnp.exp(-m_t[h]))
        sl = slice(h * DV, (h + 1) * DV)
        a_ref[:, sl] = (og_ref[:, sl].astype(F32) * hout).astype(BF16)


def _mlstm_chunk(u, gates, b_gates, batch, seq):
    t = u.shape[0]
    L = L_MLSTM
    nc = seq // L
    qk = MLSTM_HEADS * MLSTM_DK
    row = lambda col: (lambda b, c: (b * nc + c, col))
    return pl.pallas_call(
        _mlstm_chunk_body,
        grid=(batch, nc),
        in_specs=[pl.BlockSpec((L, qk), row(0)), pl.BlockSpec((L, qk), row(1)),
                  pl.BlockSpec((L, MLSTM_INNER), row(1)), pl.BlockSpec((L, MLSTM_INNER), row(2)),
                  pl.BlockSpec((L, LANES), row(0)),
                  pl.BlockSpec((1, LANES), lambda b, c: (0, 0))],
        out_specs=pl.BlockSpec((L, MLSTM_INNER), row(0)),
        out_shape=jax.ShapeDtypeStruct((t, MLSTM_INNER), BF16),
        scratch_shapes=[pltpu.VMEM((MLSTM_HEADS, MLSTM_DK, MLSTM_DV + LANES), F32),
                        pltpu.VMEM((MLSTM_HEADS, 8, LANES), F32)],
        compiler_params=_params("parallel", "arbitrary"),
        name="mlstm_chunk",
    )(u, u, u, u, gates, b_gates)


def _mla_weights(w_in, w_q_b, w_kv_b):
    half = MLA_ROPE // 2
    c_q = w_in[:, :MLA_Q_LORA]
    c_kv = w_in[:, MLA_Q_LORA:MLA_Q_LORA + MLA_KV_LORA]
    r0 = MLA_Q_LORA + MLA_KV_LORA
    k1 = w_in[:, r0:r0 + half]
    k2 = w_in[:, r0 + half:r0 + MLA_ROPE]
    z = w_in[:, r0 + MLA_ROPE:]
    pad = jnp.zeros((D_MODEL, LANES - 4 * half), w_in.dtype)
    win = jnp.concatenate([c_kv, c_q, pad, k1, k2, k2, k1, z], axis=1).astype(BF16)

    wq = w_q_b.reshape(MLA_Q_LORA, MLA_HEADS, MLA_NOPE + MLA_ROPE)
    r1 = wq[..., MLA_NOPE:MLA_NOPE + half]
    r2 = wq[..., MLA_NOPE + half:]
    nope, rot = wq[..., :MLA_NOPE], jnp.concatenate([r1, r2, r2, r1], axis=-1)
    even = jnp.concatenate([nope, rot], axis=-1)[:, 0::2]
    odd = jnp.concatenate([rot, nope], axis=-1)[:, 1::2]
    wq = jnp.stack([even, odd], axis=2).reshape(MLA_Q_LORA, MLA_HEADS * HEAD_PAD).astype(BF16)

    wkv = w_kv_b.reshape(MLA_KV_LORA, MLA_HEADS, MLA_NOPE + MLA_V)
    wk = wkv[..., :MLA_NOPE].reshape(MLA_KV_LORA, MLA_HEADS * MLA_NOPE).astype(BF16)
    wvt =wkv[..., MLA_NOPE:].reshape(MLA_KV_LORA, MLA_HEADS * MLA_V).T.astype(BF16)
    return win, wq, wk, wvt


def _rope_tables(positions):
    half = MLA_ROPE // 2
    inv = ROPE_THETA ** (-jnp.arange(0, MLA_ROPE, 2, dtype=F32) / MLA_ROPE)
    inv_lane = jnp.concatenate([jnp.zeros((MLA_NOPE,), F32), jnp.tile(inv, 4)])
    ang = positions.astype(F32).reshape(-1)[:, None] * inv_lane[None, :]
    grp = ((jnp.arange(LANES) - MLA_NOPE) // half)[None, :]
    sin = jnp.sin(ang)
    rot = jnp.where(grp < 2, jnp.cos(ang), jnp.where(grp == 2, -sin, sin))
    scale = (MLA_NOPE + MLA_ROPE) ** -0.5 * LOG2E
    tq = jnp.where(grp < 0, 1.0, rot) * scale
    tk = jnp.where(grp < 0, 0.0, rot)
    return tq, tk


def kernel(x, p, positions, norm_g, mla_w_in, mla_q_norm, mla_w_q_b, mla_kv_norm, mla_w_kv_b, mla_w_out,
           conv_w_in, conv_w, conv_w_out, mlstm_w_in, mlstm_b_gates, mlstm_w_out, ple_proj, ple_gate,
           final_norm):
    batch, seq, d = x.shape
    t = batch * seq
    xs = x.reshape(t, d)
    p_all = p.reshape(DEPTH, t, PLE_DIM)
    tq_tab, tk_tab = _rope_tables(positions)
    fn =final_norm.reshape(1, d)
    for i in range(DEPTH):
        kind, j = i % 3, i // 3
        g = norm_g[i].reshape(1, d)
        if kind == 0:
            win, wq, wk, wvt = _mla_weights(mla_w_in[j], mla_w_q_b[j], mla_w_kv_b[j])
            q, k, vt, zs = _mla_proj(xs, g, win, mla_q_norm[j].reshape(1, -1), wq,
                                     mla_kv_norm[j].reshape(1, -1), wk, wvt, tq_tab, tk_tab)
            a = _mla_attn(q, k, vt, zs, batch, seq)
            w_out = mla_w_out[j]
        elif kind == 1:
            a = _conv_mix(xs, g, conv_w_in[j].astype(BF16), conv_w[j], batch, seq)
            w_out = conv_w_out[j]
        else:
            w = mlstm_w_in[j]
            n_main = w.shape[1] - 2 * MLSTM_HEADS
            wg = jnp.pad(w[:, n_main:], ((0, 0), (0, LANES - 2 * MLSTM_HEADS))).astype(BF16)
            bg = jnp.pad(mlstm_b_gates[j], (0, LANES - 2 * MLSTM_HEADS)).reshape(1, LANES)
            u, gates = _mlstm_proj(xs, g, w.astype(BF16), wg)
            a = _mlstm_chunk(u, gates, bg, batch, seq)
            w_out = mlstm_w_out[j]
        xs = _out_ple(a, w_out.astype(BF16), xs, p_all, i, ple_proj[i].astype(BF16),
                      ple_gate[i].astype(BF16), fn, final=(i == DEPTH - 1))
    return xs.reshape(batch, seq, d)
```

```python
import functools

import jax
import jax.numpy as jnp
from jax import lax
from jax.experimental import pallas as pl
from jax.experimental.pallas import tpu as pltpu

F32 = jnp.float32
BF16 = jnp.bfloat16

D_MODEL = 1024
DEPTH = 4
CHUNK = 64
PLE_DIM = 256
EPS = 1e-6
MLA_HEADS = 16
MLA_Q_LORA = 384
MLA_KV_LORA = 256
MLA_NOPE = 64
MLA_ROPE = 32
MLA_V = 64
ROPE_THETA = 10000.0
CONV_DIM = D_MODEL
MLSTM_HEADS = 4
MLSTM_INNER = 2 * D_MODEL
MLSTM_DV = MLSTM_INNER // MLSTM_HEADS
MLSTM_DK = MLSTM_DV // 2
MLSTM_U = 2 * MLSTM_HEADS * MLSTM_DK + 2 * MLSTM_INNER

LANES = 128
MXU_N = 256
HEAD_PAD = 128
NEG = -1e30
LOG2E = 1.4426950408889634
VMEM_LIMIT = 48 * 1024 * 1024

TM_OUT = 1024
TM_MLA = 512
MLA_ROWS = 256
TQ = 256
ATTN_HEADS = 4
ATTN_LAG = 5
TM_CONV = 512
CONV_ROWS = 256
TM_MLSTM = 512
MLSTM_ROWS = 256
TN_MLSTM = 512
L_MLSTM = 256
MLSTM_SEQS = 1


def _rms(x):
    return x * lax.rsqrt(jnp.mean(x * x, axis=-1, keepdims=True) + EPS)


def _dot(a, b):
    return jnp.dot(a, b, preferred_element_type=F32)


def _dot_nt(a, b):
    return lax.dot_general(a, b, (((1,), (1,)), ((), ())), preferred_element_type=F32)


def _silu(z):
    return z * jax.nn.sigmoid(z)


def _params(*sem, flags=None):
    return pltpu.CompilerParams(dimension_semantics=sem, vmem_limit_bytes=VMEM_LIMIT, flags=flags)


def _out_ple_body(a_ref, w_ref, x_ref, p_ref, pp_ref, pg_ref, fn_ref, o_ref, *, final):
    x1 = x_ref[...] + _dot(a_ref[...], w_ref[...])
    gate = jax.nn.sigmoid(_dot(_rms(x1).astype(BF16), pg_ref[...]))
    x2 = x1 + gate * _dot(p_ref[...].astype(BF16), pp_ref[...])
    if final:
        x2 = _rms(x2) * fn_ref[...]
    o_ref[...] = x2


def _out_ple(a, w_out, x, p_all, layer, ple_proj, ple_gate, final_norm, final):
    t, k = a.shape
    tm = TM_OUT
    const = lambda i: (0, 0)
    return pl.pallas_call(
        functools.partial(_out_ple_body, final=final),
        grid=(t // tm,),
        in_specs=[
            pl.BlockSpec((tm, k), lambda i: (i, 0)),
            pl.BlockSpec((k, D_MODEL), const),
            pl.BlockSpec((tm, D_MODEL), lambda i: (i, 0)),
            pl.BlockSpec((None, tm, PLE_DIM), lambda i: (layer, i, 0)),
            pl.BlockSpec((PLE_DIM, D_MODEL), const),
            pl.BlockSpec((D_MODEL, D_MODEL), const),
            pl.BlockSpec((1, D_MODEL), const),
        ],
        out_specs=pl.BlockSpec((tm, D_MODEL), lambda i: (i, 0)),
        out_shape=jax.ShapeDtypeStruct((t, D_MODEL), F32),
        compiler_params=_params("parallel"),
        name="out_ple",
    )(a, w_out, x, p_all, ple_proj, ple_gate, final_norm)


def _mla_proj_body(x_ref, g_ref, win_ref, qn_ref, wqt_ref, kvn_ref, wk_ref, wvt_ref,
                   tqt_ref, tk_ref, qt_ref, k_ref, vt_ref, zs_ref):
    z0 = MLA_KV_LORA + 512
    groups = [slice(r, r + MLA_ROWS) for r in range(0, x_ref.shape[0], MLA_ROWS)]
    st = [dict() for _ in groups]

    def stage_norm(n):
        rows = groups[n]
        st[n]["h"] = (_rms(x_ref[rows, :]) * g_ref[...]).astype(BF16)

    def stage_latents(n):
        rows, h = groups[n], st[n]["h"]
        ckv = _dot(h, win_ref[:, 0:MLA_KV_LORA])
        u2 = _dot(h, win_ref[:, MLA_KV_LORA:MLA_KV_LORA + 512])
        cq = u2[:, 0:MLA_Q_LORA]
        kr = u2[:, MLA_Q_LORA:MLA_Q_LORA + LANES]
        st[n]["cqn"] = (_rms(cq) * qn_ref[...]).astype(BF16)
        st[n]["ckvn"] = (_rms(ckv) * kvn_ref[...]).astype(BF16)
        lane = lax.broadcasted_iota(jnp.int32, kr.shape, 1)
        quarter = MLA_ROPE
        g4 = kr * tk_ref[rows, :]
        kr_hi = g4 + jnp.where(lane < MLA_NOPE + quarter, pltpu.roll(g4, LANES - quarter, 1),
                               pltpu.roll(g4, quarter, 1))
        st[n]["kr_hi"] = kr_hi
        st[n]["kr_lo"] = pltpu.roll(kr_hi, MLA_NOPE, 1)
        st[n]["low"] = lane < MLA_NOPE

    def stage_gate(n):
        rows, h = groups[n], st[n]["h"]
        for j in range(MLA_HEADS * MLA_V // MXU_N):
            sl = slice(j * MXU_N, (j + 1) * MXU_N)
            z = _dot(h, win_ref[:, z0 + j * MXU_N:z0 + (j + 1) * MXU_N])
            zs_ref[rows, sl] = _silu(z).astype(BF16)

    def stage_heads(n):
        rows, g = groups[n], st[n]
        tqt = tqt_ref[:, rows]
        tq2 = jnp.concatenate([tqt, tqt[MLA_NOPE:], tqt[:MLA_NOPE]], axis=0)
        for j in range(MLA_HEADS * HEAD_PAD // MXU_N):
            sl = slice(j * MXU_N, (j + 1) * MXU_N)
            qt_ref[sl, rows] = (_dot_nt(wqt_ref[sl, :], g["cqn"]) * tq2).astype(BF16)
        for c in range(MLA_HEADS * MLA_NOPE // MXU_N):
            kn = _dot(g["ckvn"], wk_ref[:, c * MXU_N:(c + 1) * MXU_N])
            for half in range(2):
                pair = kn[:, half * LANES:(half + 1) * LANES]
                col = (2 * c + half) * MXU_N
                k_ref[rows, col:col + LANES] = jnp.where(g["low"], pair, g["kr_hi"]).astype(BF16)
                k_ref[rows, col + LANES:col + MXU_N] = jnp.where(g["low"], g["kr_lo"], pair).astype(BF16)
        for j in range(MLA_HEADS * MLA_V // MXU_N):
            sl = slice(j * MXU_N, (j + 1) * MXU_N)
            vt_ref[sl, rows] = _dot_nt(wvt_ref[sl, :], g["ckvn"]).astype(BF16)
        st[n] = None

    stage_norm(0)
    stage_latents(0)
    for n in range(len(groups)):
        if n + 1 < len(groups):
            stage_norm(n + 1)
        stage_gate(n)
        if n + 1 < len(groups):
            stage_latents(n + 1)
        stage_heads(n)


def _mla_proj(x, g, win, qn, wqt, kvn, wk, wvt, tqt_tab, tk_tab):
    t = x.shape[0]
    tm = TM_MLA
    hw = MLA_HEADS * HEAD_PAD
    vw = MLA_HEADS * MLA_V
    const = lambda i: (0, 0)
    row = lambda i: (i, 0)
    col = lambda i: (0, i)
    full = lambda arr: pl.BlockSpec(arr.shape, const)
    return pl.pallas_call(
        _mla_proj_body,
        grid=(t // tm,),
        in_specs=[pl.BlockSpec((tm, D_MODEL), row), full(g), full(win), full(qn), full(wqt), full(kvn),
                  full(wk), full(wvt),
                  pl.BlockSpec((LANES, tm), col), pl.BlockSpec((tm, LANES), row)],
        out_specs=[pl.BlockSpec((hw, tm), col), pl.BlockSpec((tm, hw), row),
                   pl.BlockSpec((vw, tm), col), pl.BlockSpec((tm, vw), row)],
        out_shape=[jax.ShapeDtypeStruct((hw, t), BF16), jax.ShapeDtypeStruct((t, hw), BF16),
                   jax.ShapeDtypeStruct((vw, t), BF16), jax.ShapeDtypeStruct((t, vw), BF16)],
        compiler_params=_params("parallel"),
        name="mla_proj",
    )(x, g, win, qn, wqt, kvn, wk, wvt, tqt_tab, tk_tab)


def _mla_attn_body(qt_ref, k_ref, vt_ref, zs_ref, a_ref):
    seq = k_ref.shape[0]
    nq = seq // TQ
    key_c = lax.broadcasted_iota(jnp.int32, (TQ, TQ), 0) // CHUNK
    qry_c = lax.broadcasted_iota(jnp.int32, (TQ, TQ), 1) // CHUNK
    diag_ok = key_c <= qry_c
    units = [(i, hh, j) for i in range(nq) for hh in range(ATTN_HEADS) for j in range(i + 1)]
    lead = nq + 2
    lag = ATTN_LAG
    scores, probs = {}, {}
    m_run, l_run, acc = {}, {}, {}
    outs = {}

    def score_unit(u):
        i, hh, j = units[u]
        hs = slice(hh * HEAD_PAD, (hh + 1) * HEAD_PAD)
        s = _dot(k_ref[j * TQ:(j + 1) * TQ, hs], qt_ref[hs, i * TQ:(i + 1) * TQ])
        if j == i:
            s = jnp.where(diag_ok, s, NEG)
        mj = jnp.max(s, axis=0, keepdims=True)
        m_run[i, hh] = mj if j == 0 else jnp.maximum(m_run[i, hh], mj)
        scores[u] = s

    def exp_unit(u):
        i, hh, j = units[u]
        p = jnp.exp2(scores.pop(u) - m_run[i, hh])
        lj = jnp.sum(p, axis=0, keepdims=True)
        l_run[i, hh] = lj if j == 0 else l_run[i, hh] + lj
        probs[u] = p.astype(BF16)

    def value_unit(u):
        i, hh, j = units[u]
        oj = _dot(vt_ref[hh * MLA_V:(hh + 1) * MLA_V, j * TQ:(j + 1) * TQ], probs.pop(u))
        acc[i, hh] = oj if j == 0 else acc[i, hh] + oj

    def finish_unit(u):
        i, hh, j = units[u]
        if j != i:
            return
        outs[hh] = acc.pop((i, hh)) * (1.0 / l_run.pop((i, hh)))
        if hh % 2 == 1:
            rows = slice(i * TQ, (i + 1) * TQ)
            cols = slice((hh - 1) * MLA_V, (hh + 1) * MLA_V)
            o = jnp.concatenate([outs.pop(hh - 1), outs.pop(hh)], axis=0).T
            a_ref[rows, cols] = (o * zs_ref[rows, cols].astype(F32)).astype(BF16)

    n_units = len(units)
    for t in range(-lead, n_units + lag + 3):
        if 0 <= t + lead < n_units:
            score_unit(t + lead)
        if 0 <= t - lag < n_units:
            value_unit(t - lag)
        if 0 <= t < n_units:
            exp_unit(t)
        if 0 <= t - lag - 2 < n_units:
            finish_unit(t - lag - 2)


def _mla_attn(qt, k, vt, zs, batch, seq):
    t = k.shape[0]
    nh = ATTN_HEADS
    blk = lambda w: pl.BlockSpec((seq, w), lambda b, h: (b, h))
    blk_t = lambda w: pl.BlockSpec((w, seq), lambda b, h: (h, b))
    return pl.pallas_call(
        _mla_attn_body,
        grid=(batch, MLA_HEADS // nh),
        in_specs=[blk_t(nh * HEAD_PAD), blk(nh * HEAD_PAD), blk_t(nh * MLA_V), blk(nh * MLA_V)],
        out_specs=blk(nh * MLA_V),
        out_shape=jax.ShapeDtypeStruct((t, MLA_HEADS * MLA_V), BF16),
        compiler_params=_params("parallel", "parallel"),
        name="mla_attn",
    )(qt, k, vt, zs)


def _conv_body(x_ref, g_ref, win_ref, cw_ref, a_ref, ext_ref):
    tm = TM_CONV
    t = pl.program_id(1)

    @pl.when(t == 0)
    def _():
        ext_ref[0:8, :] = jnp.zeros((8, CONV_DIM), F32)

    @pl.when(t > 0)
    def _():
        ext_ref[0:8, :] = ext_ref[tm:tm + 8, :]

    norm = lambda r: (_rms(x_ref[r:r + CONV_ROWS, :]) * g_ref[...]).astype(BF16)
    h_next = norm(0)
    for r in range(0, tm, CONV_ROWS):
        h = h_next
        for c in range(CONV_DIM // MXU_N):
            sl = slice(c * MXU_N, (c + 1) * MXU_N)
            col = lambda part: slice(part * CONV_DIM + c * MXU_N, part * CONV_DIM + (c + 1) * MXU_N)
            prod = _dot(h, win_ref[:, col(1)]) * _dot(h, win_ref[:, col(2)])
            ext_ref[8 + r:8 + r + CONV_ROWS, sl] = prod
            y = (cw_ref[2:3, sl] * prod + cw_ref[1:2, sl] * ext_ref[7 + r:7 + r + CONV_ROWS, sl]
                 + cw_ref[0:1, sl] * ext_ref[6 + r:6 + r + CONV_ROWS, sl])
            bg = _dot(h, win_ref[:, col(0)])
            z = _dot(h, win_ref[:, col(3)])
            a_ref[r:r + CONV_ROWS, sl] = (bg * y * _silu(z)).astype(BF16)
            if c == 0 and r + CONV_ROWS < tm:
                h_next = norm(r + CONV_ROWS)


def _conv_mix(x, g, win, cw, batch, seq):
    t = x.shape[0]
    tm = TM_CONV
    nt = seq // tm
    const = lambda b, i: (0, 0)
    row = lambda b, i: (b * nt + i, 0)
    return pl.pallas_call(
        _conv_body,
        grid=(batch, nt),
        in_specs=[pl.BlockSpec((tm, D_MODEL), row), pl.BlockSpec(g.shape, const),
                  pl.BlockSpec(win.shape, const), pl.BlockSpec(cw.shape, const)],
        out_specs=pl.BlockSpec((tm, CONV_DIM), row),
        out_shape=jax.ShapeDtypeStruct((t, CONV_DIM), BF16),
        scratch_shapes=[pltpu.VMEM((tm + 8, CONV_DIM), F32)],
        compiler_params=_params("parallel", "arbitrary"),
        name="conv_mix",
    )(x, g, win, cw)


def _mlstm_proj_body(x_ref, g_ref, w_ref, wg_ref, u_ref, gates_ref):
    tn = TN_MLSTM
    qk = MLSTM_HEADS * MLSTM_DK
    cols = lambda start, j: slice(start + j * tn, start + (j + 1) * tn)
    norm = lambda r: (_rms(x_ref[r:r + MLSTM_ROWS, :]) * g_ref[...]).astype(BF16)
    h_next = norm(0)
    for r in range(0, x_ref.shape[0], MLSTM_ROWS):
        h = h_next
        rows = slice(r, r + MLSTM_ROWS)
        proj = lambda sl: _dot(h, w_ref[:, sl])
        for j in range(MLSTM_INNER // tn):
            o = proj(cols(2 * qk + MLSTM_INNER, j))
            z = proj(cols(2 * qk + 2 * MLSTM_INNER, j))
            u_ref[rows, cols(2 * qk + MLSTM_INNER, j)] = (jax.nn.sigmoid(o) * _silu(z)).astype(BF16)
            if j == 0 and r + MLSTM_ROWS < x_ref.shape[0]:
                h_next = norm(r + MLSTM_ROWS)
        gates_ref[rows, :] = _dot(h, wg_ref[...])
        for j in range(MLSTM_INNER // tn):
            u_ref[rows, cols(2 * qk, j)] = proj(cols(2 * qk, j)).astype(BF16)
        for j in range(qk // tn):
            u_ref[rows, cols(0, j)] = proj(cols(0, j)).astype(BF16)
            u_ref[rows, cols(qk, j)] = (proj(cols(qk, j)) * (MLSTM_DK ** -0.5)).astype(BF16)


def _mlstm_proj(x, g, w, wg):
    t = x.shape[0]
    n = w.shape[1] - 2 * MLSTM_HEADS
    tm = TM_MLSTM
    const = lambda i: (0, 0)
    row = lambda i: (i, 0)
    return pl.pallas_call(
        _mlstm_proj_body,
        grid=(t // tm,),
        in_specs=[pl.BlockSpec((tm, D_MODEL), row),
                  pl.BlockSpec(g.shape, const),
                  pl.BlockSpec((D_MODEL, n), const, pipeline_mode=pl.Buffered(1)),
                  pl.BlockSpec(wg.shape, const)],
        out_specs=[pl.BlockSpec((tm, MLSTM_U), row), pl.BlockSpec((tm, LANES), row)],
        out_shape=[jax.ShapeDtypeStruct((t, MLSTM_U), BF16), jax.ShapeDtypeStruct((t, LANES), F32)],
        compiler_params=_params("parallel"),
        name="mlstm_proj",
    )(x, g, w, wg)


def _mlstm_chunk_body(q_ref, k_ref, v_ref, og_ref, g_ref, bg_ref, a_ref, c_scr, m_scr):
    L = L_MLSTM
    H, DK, DV = MLSTM_HEADS, MLSTM_DK, MLSTM_DV
    NB = q_ref.shape[0]

    @pl.when(pl.program_id(1) == 0)
    def _():
        c_scr[...] = jnp.zeros_like(c_scr)
        m_scr[...] = jnp.zeros_like(m_scr)

    r_i = lax.broadcasted_iota(jnp.int32, (L, L), 0)
    c_i = lax.broadcasted_iota(jnp.int32, (L, L), 1)
    causal = c_i <= r_i
    tril = causal.astype(BF16)
    gates, b_all, b_t, g_t = [], [], [], []
    for b in range(NB):
        g = g_ref[b] + bg_ref[...]
        lf = jnp.minimum(g, 0.0) - jnp.log(1.0 + jnp.exp(-jnp.abs(g)))
        hi = lf.astype(BF16)
        r1 = lf - hi.astype(F32)
        mid = r1.astype(BF16)
        lo = (r1 - mid.astype(F32)).astype(BF16)
        cs = _dot(tril, hi) + _dot(tril, mid) + _dot(tril, lo)
        gates.append(g)
        b_all.append(cs)
        b_t.append(cs.T)
        g_t.append(g.T)

    inst = [(b, h) for b in range(NB) for h in range(H)]
    ids = range(len(inst))
    ones = jnp.ones((L, LANES), BF16)
    q = [q_ref[b, :, h * DK:(h + 1) * DK] for b, h in inst]
    k = [k_ref[b, :, h * DK:(h + 1) * DK] for b, h in inst]
    v = [jnp.concatenate([v_ref[b, :, h * DV:(h + 1) * DV], ones], axis=1) for b, h in inst]
    bc = [b_all[b][:, H + h:H + h + 1] for b, h in inst]
    igc = [gates[b][:, h:h + 1] for b, h in inst]
    m = [m_scr[i][0:1, 0:1] for i in ids]
    c_old = [c_scr[i] for i in ids]

    qk = [_dot_nt(q[i], k[i]) for i in ids]

    upd, decay, m_new = [], [], []
    for i in ids:
        b_last = bc[i][L - 1:L, :]
        d_state = b_last - bc[i] + igc[i]
        m_new.append(jnp.maximum(b_last + m[i], jnp.max(d_state, axis=0, keepdims=True)))
        w_s = jnp.exp(d_state - m_new[i])
        decay.append(jnp.exp(b_last + m[i] - m_new[i]))
        kw = k[i].astype(F32) * w_s
        upd.append(_dot(kw.T.astype(BF16), v[i]))

    d_log, gm, m_t = [], [], []
    for i, (b, h) in enumerate(inst):
        c_scr[i] = decay[i] * c_old[i] + upd[i]
        m_scr[i] = jnp.broadcast_to(m_new[i], (8, LANES))
        br = b_t[b][H + h:H + h + 1, :]
        igr = g_t[b][h:h + 1, :]
        d_log.append(jnp.where(causal, bc[i] - br + igr, NEG))
        gm.append(bc[i] + m[i])
        m_t.append(jnp.maximum(gm[i], jnp.max(d_log[i], axis=1, keepdims=True)))
    num = []
    for i in ids:
        qs = (q[i].astype(F32) * jnp.exp(gm[i] - m_t[i])).astype(BF16)
        s = (qk[i] * jnp.exp(d_log[i] - m_t[i])).astype(BF16)
        lhs = jnp.concatenate([qs, s], axis=1)
        rhs = jnp.concatenate([c_old[i].astype(BF16), v[i]], axis=0)
        num.append(_dot(lhs, rhs))
    for i, (b, h) in enumerate(inst):
        den = num[i][:, DV:DV + 1]
        hout = num[i][:, :DV] / jnp.maximum(jnp.abs(den), jnp.exp(-m_t[i]))
        sl = slice(h * DV, (h + 1) * DV)
        a_ref[b, :, sl] = (og_ref[b, :, sl].astype(F32) * hout).astype(BF16)


def _mlstm_chunk(u, gates, b_gates, batch, seq):
    t = u.shape[0]
    L, nb = L_MLSTM, MLSTM_SEQS
    qk = MLSTM_HEADS * MLSTM_DK
    u3 = u.reshape(batch, seq, u.shape[1])
    g3 = gates.reshape(batch, seq, LANES)
    blk = lambda width, col: pl.BlockSpec((nb, L, width), lambda b, c: (b, c, col))
    a = pl.pallas_call(
        _mlstm_chunk_body,
        grid=(batch // nb, seq // L),
        in_specs=[blk(qk, 0), blk(qk, 1), blk(MLSTM_INNER, 1), blk(MLSTM_INNER, 2), blk(LANES, 0),
                  pl.BlockSpec((1, LANES), lambda b, c: (0, 0))],
        out_specs=blk(MLSTM_INNER, 0),
        out_shape=jax.ShapeDtypeStruct((batch, seq, MLSTM_INNER), BF16),
        scratch_shapes=[pltpu.VMEM((nb * MLSTM_HEADS, MLSTM_DK, MLSTM_DV + LANES), F32),
                        pltpu.VMEM((nb * MLSTM_HEADS, 8, LANES), F32)],
        compiler_params=_params("parallel", "arbitrary"),
        name="mlstm_chunk",
    )(u3, u3, u3, u3, g3, b_gates)
    return a.reshape(t, MLSTM_INNER)


def _mla_weights(w_in, w_q_b, w_kv_b):
    half = MLA_ROPE // 2
    c_q = w_in[:, :MLA_Q_LORA]
    c_kv = w_in[:, MLA_Q_LORA:MLA_Q_LORA + MLA_KV_LORA]
    r0 = MLA_Q_LORA + MLA_KV_LORA
    k1 = w_in[:, r0:r0 + half]
    k2 = w_in[:, r0 + half:r0 + MLA_ROPE]
    z = w_in[:, r0 + MLA_ROPE:]
    pad = jnp.zeros((D_MODEL, LANES - 4 * half), w_in.dtype)
    win = jnp.concatenate([c_kv, c_q, pad, k1, k2, k2, k1, z], axis=1).astype(BF16)

    wq = w_q_b.reshape(MLA_Q_LORA, MLA_HEADS, MLA_NOPE + MLA_ROPE)
    r1 = wq[..., MLA_NOPE:MLA_NOPE + half]
    r2 = wq[..., MLA_NOPE + half:]
    nope, rot = wq[..., :MLA_NOPE], jnp.concatenate([r1, r2, r2, r1], axis=-1)
    even = jnp.concatenate([nope, rot], axis=-1)[:, 0::2]
    odd = jnp.concatenate([rot, nope], axis=-1)[:, 1::2]
    wqt = jnp.stack([even, odd], axis=2).reshape(MLA_Q_LORA, MLA_HEADS * HEAD_PAD).T.astype(BF16)

    wkv = w_kv_b.reshape(MLA_KV_LORA, MLA_HEADS, MLA_NOPE + MLA_V)
    wk = wkv[..., :MLA_NOPE].reshape(MLA_KV_LORA, MLA_HEADS * MLA_NOPE).astype(BF16)
    wvt =wkv[..., MLA_NOPE:].reshape(MLA_KV_LORA, MLA_HEADS * MLA_V).T.astype(BF16)
    return win, wqt, wk, wvt


def _rope_tables(positions):
    half = MLA_ROPE // 2
    inv = ROPE_THETA ** (-jnp.arange(0, MLA_ROPE, 2, dtype=F32) / MLA_ROPE)
    inv_lane = jnp.concatenate([jnp.zeros((MLA_NOPE,), F32), jnp.tile(inv, 4)])
    ang = positions.astype(F32).reshape(-1)[:, None] * inv_lane[None, :]
    grp = ((jnp.arange(LANES) - MLA_NOPE) // half)[None, :]
    sin = jnp.sin(ang)
    rot = jnp.where(grp < 2, jnp.cos(ang), jnp.where(grp == 2, -sin, sin))
    scale = (MLA_NOPE + MLA_ROPE) ** -0.5 * LOG2E
    tq = jnp.where(grp < 0, 1.0, rot) * scale
    tk = jnp.where(grp < 0, 0.0, rot)
    return tq.T, tk


def kernel(x, p, positions, norm_g, mla_w_in, mla_q_norm, mla_w_q_b, mla_kv_norm, mla_w_kv_b, mla_w_out,
           conv_w_in, conv_w, conv_w_out, mlstm_w_in, mlstm_b_gates, mlstm_w_out, ple_proj, ple_gate,
           final_norm):
    batch, seq, d = x.shape
    t = batch * seq
    xs = x.reshape(t, d)
    p_all = p.reshape(DEPTH, t, PLE_DIM)
    tqt_tab, tk_tab = _rope_tables(positions)
    fn =final_norm.reshape(1, d)
    for i in range(DEPTH):
        kind, j = i % 3, i // 3
        g = norm_g[i].reshape(1, d)
        if kind == 0:
            win, wqt, wk, wvt = _mla_weights(mla_w_in[j], mla_w_q_b[j], mla_w_kv_b[j])
            qt, k, vt, zs = _mla_proj(xs, g, win, mla_q_norm[j].reshape(1, -1), wqt,
                                      mla_kv_norm[j].reshape(1, -1), wk, wvt, tqt_tab, tk_tab)
            a = _mla_attn(qt, k, vt, zs, batch, seq)
            w_out = mla_w_out[j]
        elif kind == 1:
            a = _conv_mix(xs, g, conv_w_in[j].astype(BF16), conv_w[j], batch, seq)
            w_out = conv_w_out[j]
        else:
            w = mlstm_w_in[j]
            n_main = w.shape[1] - 2 * MLSTM_HEADS
            wg = jnp.pad(w[:, n_main:], ((0, 0), (0, LANES - 2 * MLSTM_HEADS))).astype(BF16)
            bg = jnp.pad(mlstm_b_gates[j], (0, LANES - 2 * MLSTM_HEADS)).reshape(1, LANES)
            u, gates = _mlstm_proj(xs, g, w.astype(BF16), wg)
            a = _mlstm_chunk(u, gates, bg, batch, seq)
            w_out = mlstm_w_out[j]
        xs = _out_ple(a, w_out.astype(BF16), xs, p_all, i, ple_proj[i].astype(BF16),
                      ple_gate[i].astype(BF16), fn, final=(i == DEPTH - 1))
    return xs.reshape(batch, seq, d)
```

```python
import functools

import jax
import jax.numpy as jnp
from jax import lax
from jax.experimental import pallas as pl
from jax.experimental.pallas import tpu as pltpu

F32 = jnp.float32
BF16 = jnp.bfloat16

D_MODEL = 1024
DEPTH = 4
CHUNK = 64
PLE_DIM = 256
EPS = 1e-6
MLA_HEADS = 16
MLA_Q_LORA = 384
MLA_KV_LORA = 256
MLA_NOPE = 64
MLA_ROPE = 32
MLA_V = 64
ROPE_THETA = 10000.0
CONV_DIM = D_MODEL
MLSTM_HEADS = 4
MLSTM_INNER = 2 * D_MODEL
MLSTM_DV = MLSTM_INNER // MLSTM_HEADS
MLSTM_DK = MLSTM_DV // 2
MLSTM_U = 2 * MLSTM_HEADS * MLSTM_DK + 2 * MLSTM_INNER

LANES = 128
MXU_N = 256
HEAD_PAD = 128
NEG = -1e30
LOG2E = 1.4426950408889634
VMEM_LIMIT = 48 * 1024 * 1024

TM_OUT = 1024
TM_MLA = 512
MLA_ROWS = 256
TQ = 256
ATTN_HEADS = 4
ATTN_LAG = 5
TM_CONV = 512
CONV_ROWS = 256
TM_MLSTM = 512
MLSTM_ROWS = 256
TN_MLSTM = 512
L_MLSTM = 256
MLSTM_SEQS = 1


def _rms(x):
    return x * lax.rsqrt(jnp.mean(x * x, axis=-1, keepdims=True) + EPS)


def _dot(a, b):
    return jnp.dot(a, b, preferred_element_type=F32)


def _dot_nt(a, b):
    return lax.dot_general(a, b, (((1,), (1,)), ((), ())), preferred_element_type=F32)


def _silu(z):
    return z * jax.nn.sigmoid(z)


def _params(*sem, flags=None):
    return pltpu.CompilerParams(dimension_semantics=sem, vmem_limit_bytes=VMEM_LIMIT, flags=flags)


def _out_ple_body(a_ref, w_ref, x_ref, p_ref, pp_ref, pg_ref, fn_ref, o_ref, *, final):
    x1 = x_ref[...] + _dot(a_ref[...], w_ref[...])
    gate = jax.nn.sigmoid(_dot(_rms(x1).astype(BF16), pg_ref[...]))
    x2 = x1 + gate * _dot(p_ref[...].astype(BF16), pp_ref[...])
    if final:
        x2 = _rms(x2) * fn_ref[...]
    o_ref[...] = x2


def _out_ple(a, w_out, x, p_all, layer, ple_proj, ple_gate, final_norm, final):
    t, k = a.shape
    tm = TM_OUT
    const = lambda i: (0, 0)
    return pl.pallas_call(
        functools.partial(_out_ple_body, final=final),
        grid=(t // tm,),
        in_specs=[
            pl.BlockSpec((tm, k), lambda i: (i, 0)),
            pl.BlockSpec((k, D_MODEL), const),
            pl.BlockSpec((tm, D_MODEL), lambda i: (i, 0)),
            pl.BlockSpec((None, tm, PLE_DIM), lambda i: (layer, i, 0)),
            pl.BlockSpec((PLE_DIM, D_MODEL), const),
            pl.BlockSpec((D_MODEL, D_MODEL), const),
            pl.BlockSpec((1, D_MODEL), const),
        ],
        out_specs=pl.BlockSpec((tm, D_MODEL), lambda i: (i, 0)),
        out_shape=jax.ShapeDtypeStruct((t, D_MODEL), F32),
        compiler_params=_params("parallel"),
        name="out_ple",
    )(a, w_out, x, p_all, ple_proj, ple_gate, final_norm)


def _mla_proj_body(x_ref, g_ref, win_ref, qn_ref, wqt_ref, kvn_ref, wk_ref, wvt_ref,
                   rope_ref, qt_ref, k_ref, vt_ref, zs_ref):
    z0 = MLA_KV_LORA + 512
    groups = [slice(r, r + MLA_ROWS) for r in range(0, x_ref.shape[0], MLA_ROWS)]
    st = [dict() for _ in groups]

    def stage_norm(n):
        rows = groups[n]
        st[n]["h"] = (_rms(x_ref[rows, :]) * g_ref[...]).astype(BF16)

    def stage_latents(n):
        rows, h = groups[n], st[n]["h"]
        ckv = _dot(h, win_ref[:, 0:MLA_KV_LORA])
        u2 = _dot(h, win_ref[:, MLA_KV_LORA:MLA_KV_LORA + 512])
        cq = u2[:, 0:MLA_Q_LORA]
        kr = u2[:, MLA_Q_LORA:MLA_Q_LORA + LANES]
        st[n]["cqn"] = (_rms(cq) * qn_ref[...]).astype(BF16)
        st[n]["ckvn"] = (_rms(ckv) * kvn_ref[...]).astype(BF16)
        lane = lax.broadcasted_iota(jnp.int32, kr.shape, 1)
        quarter = MLA_ROPE
        rope = rope_ref[:, rows].T
        g4 = kr * rope
        kr_hi = g4 + jnp.where(lane < MLA_NOPE + quarter, pltpu.roll(g4, LANES - quarter, 1),
                               pltpu.roll(g4, quarter, 1))
        st[n]["kr_hi"] = kr_hi
        st[n]["kr_lo"] = pltpu.roll(kr_hi, MLA_NOPE, 1)
        st[n]["low"] = lane < MLA_NOPE

    def stage_gate(n):
        rows, h = groups[n], st[n]["h"]
        for j in range(MLA_HEADS * MLA_V // MXU_N):
            sl = slice(j * MXU_N, (j + 1) * MXU_N)
            z = _dot(h, win_ref[:, z0 + j * MXU_N:z0 + (j + 1) * MXU_N])
            zs_ref[rows, sl] = _silu(z).astype(BF16)

    def stage_heads(n):
        rows, g = groups[n], st[n]
        tqt = rope_ref[:, rows] * ((MLA_NOPE + MLA_ROPE) ** -0.5 * LOG2E)
        tq2 = jnp.concatenate([tqt, tqt[MLA_NOPE:], tqt[:MLA_NOPE]], axis=0)
        for j in range(MLA_HEADS * HEAD_PAD // MXU_N):
            sl = slice(j * MXU_N, (j + 1) * MXU_N)
            qt_ref[sl, rows] = (_dot_nt(wqt_ref[sl, :], g["cqn"]) * tq2).astype(BF16)
        for c in range(MLA_HEADS * MLA_NOPE // MXU_N):
            kn = _dot(g["ckvn"], wk_ref[:, c * MXU_N:(c + 1) * MXU_N])
            for half in range(2):
                pair = kn[:, half * LANES:(half + 1) * LANES]
                col = (2 * c + half) * MXU_N
                k_ref[rows, col:col + LANES] = jnp.where(g["low"], pair, g["kr_hi"]).astype(BF16)
                k_ref[rows, col + LANES:col + MXU_N] = jnp.where(g["low"], g["kr_lo"], pair).astype(BF16)
        for j in range(MLA_HEADS * MLA_V // MXU_N):
            sl = slice(j * MXU_N, (j + 1) * MXU_N)
            vt_ref[sl, rows] = _dot_nt(wvt_ref[sl, :], g["ckvn"]).astype(BF16)
        st[n] = None

    stage_norm(0)
    stage_latents(0)
    for n in range(len(groups)):
        if n + 1 < len(groups):
            stage_norm(n + 1)
        stage_gate(n)
        if n + 1 < len(groups):
            stage_latents(n + 1)
        stage_heads(n)


def _mla_proj(x, g, win, qn, wqt, kvn, wk, wvt, rope_tab):
    t = x.shape[0]
    tm = TM_MLA
    hw = MLA_HEADS * HEAD_PAD
    vw = MLA_HEADS * MLA_V
    const = lambda i: (0, 0)
    row = lambda i: (i, 0)
    col = lambda i: (0, i)
    full = lambda arr: pl.BlockSpec(arr.shape, const)
    return pl.pallas_call(
        _mla_proj_body,
        grid=(t // tm,),
        in_specs=[pl.BlockSpec((tm, D_MODEL), row), full(g), full(win), full(qn), full(wqt), full(kvn),
                  full(wk), full(wvt),
                  pl.BlockSpec((HEAD_PAD, tm), col)],
        out_specs=[pl.BlockSpec((hw, tm), col), pl.BlockSpec((tm, hw), row),
                   pl.BlockSpec((vw, tm), col), pl.BlockSpec((tm, vw), row)],
        out_shape=[jax.ShapeDtypeStruct((hw, t), BF16), jax.ShapeDtypeStruct((t, hw), BF16),
                   jax.ShapeDtypeStruct((vw, t), BF16), jax.ShapeDtypeStruct((t, vw), BF16)],
        compiler_params=_params("parallel"),
        name="mla_proj",
    )(x, g, win, qn, wqt, kvn, wk, wvt, rope_tab)


def _mla_attn_body(qt_ref, k_ref, vt_ref, zs_ref, a_ref):
    seq = k_ref.shape[0]
    nq = seq // TQ
    key_c = lax.broadcasted_iota(jnp.int32, (TQ, TQ), 0) // CHUNK
    qry_c = lax.broadcasted_iota(jnp.int32, (TQ, TQ), 1) // CHUNK
    diag_ok = key_c <= qry_c
    units = [(i, hh, j) for i in range(nq) for hh in range(ATTN_HEADS) for j in range(i + 1)]
    lead = nq + 2
    lag = ATTN_LAG
    scores, probs = {}, {}
    m_run, l_run, acc = {}, {}, {}
    outs = {}

    def score_unit(u):
        i, hh, j = units[u]
        hs = slice(hh * HEAD_PAD, (hh + 1) * HEAD_PAD)
        s = _dot(k_ref[j * TQ:(j + 1) * TQ, hs], qt_ref[hs, i * TQ:(i + 1) * TQ])
        if j == i:
            s = jnp.where(diag_ok, s, NEG)
        mj = jnp.max(s, axis=0, keepdims=True)
        m_run[i, hh] = mj if j == 0 else jnp.maximum(m_run[i, hh], mj)
        scores[u] = s

    def exp_unit(u):
        i, hh, j = units[u]
        p = jnp.exp2(scores.pop(u) - m_run[i, hh])
        lj = jnp.sum(p, axis=0, keepdims=True)
        l_run[i, hh] = lj if j == 0 else l_run[i, hh] + lj
        probs[u] = p.astype(BF16)

    def value_unit(u):
        i, hh, j = units[u]
        oj = _dot(vt_ref[hh * MLA_V:(hh + 1) * MLA_V, j * TQ:(j + 1) * TQ], probs.pop(u))
        acc[i, hh] = oj if j == 0 else acc[i, hh] + oj

    def finish_unit(u):
        i, hh, j = units[u]
        if j != i:
            return
        outs[hh] = acc.pop((i, hh)) * (1.0 / l_run.pop((i, hh)))
        if hh % 2 == 1:
            rows = slice(i * TQ, (i + 1) * TQ)
            cols = slice((hh - 1) * MLA_V, (hh + 1) * MLA_V)
            o = jnp.concatenate([outs.pop(hh - 1), outs.pop(hh)], axis=0).T
            a_ref[rows, cols] = (o * zs_ref[rows, cols].astype(F32)).astype(BF16)

    n_units = len(units)
    for t in range(-lead, n_units + lag + 3):
        if 0 <= t + lead < n_units:
            score_unit(t + lead)
        if 0 <= t - lag < n_units:
            value_unit(t - lag)
        if 0 <= t < n_units:
            exp_unit(t)
        if 0 <= t - lag - 2 < n_units:
            finish_unit(t - lag - 2)


def _mla_attn(qt, k, vt, zs, batch, seq):
    t = k.shape[0]
    nh = ATTN_HEADS
    blk = lambda w: pl.BlockSpec((seq, w), lambda b, h: (b, h))
    blk_t = lambda w: pl.BlockSpec((w, seq), lambda b, h: (h, b))
    return pl.pallas_call(
        _mla_attn_body,
        grid=(batch, MLA_HEADS // nh),
        in_specs=[blk_t(nh * HEAD_PAD), blk(nh * HEAD_PAD), blk_t(nh * MLA_V), blk(nh * MLA_V)],
        out_specs=blk(nh * MLA_V),
        out_shape=jax.ShapeDtypeStruct((t, MLA_HEADS * MLA_V), BF16),
        compiler_params=_params("parallel", "parallel"),
        name="mla_attn",
    )(qt, k, vt, zs)


def _conv_body(x_ref, g_ref, win_ref, cw_ref, a_ref, ext_ref):
    tm = TM_CONV
    t = pl.program_id(1)

    @pl.when(t == 0)
    def _():
        ext_ref[0:8, :] = jnp.zeros((8, CONV_DIM), F32)

    @pl.when(t > 0)
    def _():
        ext_ref[0:8, :] = ext_ref[tm:tm + 8, :]

    norm = lambda r: (_rms(x_ref[r:r + CONV_ROWS, :]) * g_ref[...]).astype(BF16)
    h_next = norm(0)
    for r in range(0, tm, CONV_ROWS):
        h = h_next
        for c in range(CONV_DIM // MXU_N):
            sl = slice(c * MXU_N, (c + 1) * MXU_N)
            col = lambda part: slice(part * CONV_DIM + c * MXU_N, part * CONV_DIM + (c + 1) * MXU_N)
            prod = _dot(h, win_ref[:, col(1)]) * _dot(h, win_ref[:, col(2)])
            ext_ref[8 + r:8 + r + CONV_ROWS, sl] = prod
            y = (cw_ref[2:3, sl] * prod + cw_ref[1:2, sl] * ext_ref[7 + r:7 + r + CONV_ROWS, sl]
                 + cw_ref[0:1, sl] * ext_ref[6 + r:6 + r + CONV_ROWS, sl])
            bg = _dot(h, win_ref[:, col(0)])
            z = _dot(h, win_ref[:, col(3)])
            a_ref[r:r + CONV_ROWS, sl] = (bg * y * _silu(z)).astype(BF16)
            if c == 0 and r + CONV_ROWS < tm:
                h_next = norm(r + CONV_ROWS)


def _conv_mix(x, g, win, cw, batch, seq):
    t = x.shape[0]
    tm = TM_CONV
    nt = seq // tm
    const = lambda b, i: (0, 0)
    row = lambda b, i: (b * nt + i, 0)
    return pl.pallas_call(
        _conv_body,
        grid=(batch, nt),
        in_specs=[pl.BlockSpec((tm, D_MODEL), row), pl.BlockSpec(g.shape, const),
                  pl.BlockSpec(win.shape, const), pl.BlockSpec(cw.shape, const)],
        out_specs=pl.BlockSpec((tm, CONV_DIM), row),
        out_shape=jax.ShapeDtypeStruct((t, CONV_DIM), BF16),
        scratch_shapes=[pltpu.VMEM((tm + 8, CONV_DIM), F32)],
        compiler_params=_params("parallel", "arbitrary"),
        name="conv_mix",
    )(x, g, win, cw)


def _mlstm_proj_body(x_ref, g_ref, w_ref, wg_ref, u_ref, gates_ref):
    tn = TN_MLSTM
    qk = MLSTM_HEADS * MLSTM_DK
    cols = lambda start, j: slice(start + j * tn, start + (j + 1) * tn)
    norm = lambda r: (_rms(x_ref[r:r + MLSTM_ROWS, :]) * g_ref[...]).astype(BF16)
    h_next = norm(0)
    for r in range(0, x_ref.shape[0], MLSTM_ROWS):
        h = h_next
        rows = slice(r, r + MLSTM_ROWS)
        proj = lambda sl: _dot(h, w_ref[:, sl])
        for j in range(MLSTM_INNER // tn):
            o = proj(cols(2 * qk + MLSTM_INNER, j))
            z = proj(cols(2 * qk + 2 * MLSTM_INNER, j))
            u_ref[rows, cols(2 * qk + MLSTM_INNER, j)] = (jax.nn.sigmoid(o) * _silu(z)).astype(BF16)
            if j == 0 and r + MLSTM_ROWS < x_ref.shape[0]:
                h_next = norm(r + MLSTM_ROWS)
        gates_ref[rows, :] = _dot(h, wg_ref[...])
        for j in range(MLSTM_INNER // tn):
            u_ref[rows, cols(2 * qk, j)] = proj(cols(2 * qk, j)).astype(BF16)
        for j in range(qk // tn):
            u_ref[rows, cols(0, j)] = proj(cols(0, j)).astype(BF16)
            u_ref[rows, cols(qk, j)] = (proj(cols(qk, j)) * (MLSTM_DK ** -0.5)).astype(BF16)


def _mlstm_proj(x, g, w, wg):
    t = x.shape[0]
    n = w.shape[1] - 2 * MLSTM_HEADS
    tm = TM_MLSTM
    const = lambda i: (0, 0)
    row = lambda i: (i, 0)
    return pl.pallas_call(
        _mlstm_proj_body,
        grid=(t // tm,),
        in_specs=[pl.BlockSpec((tm, D_MODEL), row),
                  pl.BlockSpec(g.shape, const),
                  pl.BlockSpec((D_MODEL, n), const, pipeline_mode=pl.Buffered(1)),
                  pl.BlockSpec(wg.shape, const)],
        out_specs=[pl.BlockSpec((tm, MLSTM_U), row), pl.BlockSpec((tm, LANES), row)],
        out_shape=[jax.ShapeDtypeStruct((t, MLSTM_U), BF16), jax.ShapeDtypeStruct((t, LANES), F32)],
        compiler_params=_params("parallel"),
        name="mlstm_proj",
    )(x, g, w, wg)


def _mlstm_chunk_body(q_ref, k_ref, v_ref, og_ref, g_ref, bg_ref, a_ref, c_scr, m_scr):
    L = L_MLSTM
    H, DK, DV = MLSTM_HEADS, MLSTM_DK, MLSTM_DV
    NB = q_ref.shape[0]

    @pl.when(pl.program_id(1) == 0)
    def _():
        c_scr[...] = jnp.zeros_like(c_scr)
        m_scr[...] = jnp.zeros_like(m_scr)

    r_i = lax.broadcasted_iota(jnp.int32, (L, L), 0)
    c_i = lax.broadcasted_iota(jnp.int32, (L, L), 1)
    causal = c_i <= r_i
    tril = causal.astype(BF16)
    gates, b_all, b_t, g_t = [], [], [], []
    for b in range(NB):
        g = g_ref[b] + bg_ref[...]
        lf = jnp.minimum(g, 0.0) - jnp.log(1.0 + jnp.exp(-jnp.abs(g)))
        hi = lf.astype(BF16)
        r1 = lf - hi.astype(F32)
        mid = r1.astype(BF16)
        lo = (r1 - mid.astype(F32)).astype(BF16)
        cs = _dot(tril, hi) + _dot(tril, mid) + _dot(tril, lo)
        gates.append(g)
        b_all.append(cs)
        b_t.append(cs.T)
        g_t.append(g.T)

    inst = [(b, h) for b in range(NB) for h in range(H)]
    ids = range(len(inst))
    ones = jnp.ones((L, LANES), BF16)
    q = [q_ref[b, :, h * DK:(h + 1) * DK] for b, h in inst]
    k = [k_ref[b, :, h * DK:(h + 1) * DK] for b, h in inst]
    v = [jnp.concatenate([v_ref[b, :, h * DV:(h + 1) * DV], ones], axis=1) for b, h in inst]
    bc = [b_all[b][:, H + h:H + h + 1] for b, h in inst]
    igc = [gates[b][:, h:h + 1] for b, h in inst]
    m = [m_scr[i][0:1, 0:1] for i in ids]
    c_old = [c_scr[i] for i in ids]

    qk = [_dot_nt(q[i], k[i]) for i in ids]

    upd, decay, m_new = [], [], []
    for i in ids:
        b_last = bc[i][L - 1:L, :]
        d_state = b_last - bc[i] + igc[i]
        m_new.append(jnp.maximum(b_last + m[i], jnp.max(d_state, axis=0, keepdims=True)))
        w_s = jnp.exp(d_state - m_new[i])
        decay.append(jnp.exp(b_last + m[i] - m_new[i]))
        kw = k[i].astype(F32) * w_s
        upd.append(_dot(kw.T.astype(BF16), v[i]))

    d_log, gm, m_t = [], [], []
    for i, (b, h) in enumerate(inst):
        c_scr[i] = decay[i] * c_old[i] + upd[i]
        m_scr[i] = jnp.broadcast_to(m_new[i], (8, LANES))
        br = b_t[b][H + h:H + h + 1, :]
        igr = g_t[b][h:h + 1, :]
        d_log.append(jnp.where(causal, bc[i] - br + igr, NEG))
        gm.append(bc[i] + m[i])
        m_t.append(jnp.maximum(gm[i], jnp.max(d_log[i], axis=1, keepdims=True)))
    num = []
    for i in ids:
        qs = (q[i].astype(F32) * jnp.exp(gm[i] - m_t[i])).astype(BF16)
        s = (qk[i] * jnp.exp(d_log[i] - m_t[i])).astype(BF16)
        lhs = jnp.concatenate([qs, s], axis=1)
        rhs = jnp.concatenate([c_old[i].astype(BF16), v[i]], axis=0)
        num.append(_dot(lhs, rhs))
    for i, (b, h) in enumerate(inst):
        den = num[i][:, DV:DV + 1]
        hout = num[i][:, :DV] / jnp.maximum(jnp.abs(den), jnp.exp(-m_t[i]))
        sl = slice(h * DV, (h + 1) * DV)
        a_ref[b, :, sl] = (og_ref[b, :, sl].astype(F32) * hout).astype(BF16)


def _mlstm_chunk(u, gates, b_gates, batch, seq):
    t = u.shape[0]
    L, nb = L_MLSTM, MLSTM_SEQS
    qk = MLSTM_HEADS * MLSTM_DK
    u3 = u.reshape(batch, seq, u.shape[1])
    g3 = gates.reshape(batch, seq, LANES)
    blk = lambda width, col: pl.BlockSpec((nb, L, width), lambda b, c: (b, c, col))
    a = pl.pallas_call(
        _mlstm_chunk_body,
        grid=(batch // nb, seq // L),
        in_specs=[blk(qk, 0), blk(qk, 1), blk(MLSTM_INNER, 1), blk(MLSTM_INNER, 2), blk(LANES, 0),
                  pl.BlockSpec((1, LANES), lambda b, c: (0, 0))],
        out_specs=blk(MLSTM_INNER, 0),
        out_shape=jax.ShapeDtypeStruct((batch, seq, MLSTM_INNER), BF16),
        scratch_shapes=[pltpu.VMEM((nb * MLSTM_HEADS, MLSTM_DK, MLSTM_DV + LANES), F32),
                        pltpu.VMEM((nb * MLSTM_HEADS, 8, LANES), F32)],
        compiler_params=_params("parallel", "arbitrary"),
        name="mlstm_chunk",
    )(u3, u3, u3, u3, g3, b_gates)
    return a.reshape(t, MLSTM_INNER)


def _mla_weights(w_in, w_q_b, w_kv_b):
    half = MLA_ROPE // 2
    c_q = w_in[:, :MLA_Q_LORA]
    c_kv = w_in[:, MLA_Q_LORA:MLA_Q_LORA + MLA_KV_LORA]
    r0 = MLA_Q_LORA + MLA_KV_LORA
    k1 = w_in[:, r0:r0 + half]
    k2 = w_in[:, r0 + half:r0 + MLA_ROPE]
    z = w_in[:, r0 + MLA_ROPE:]
    pad = jnp.zeros((D_MODEL, LANES - 4 * half), w_in.dtype)
    win = jnp.concatenate([c_kv, c_q, pad, k1, k2, k2, k1, z], axis=1).astype(BF16)

    wq = w_q_b.reshape(MLA_Q_LORA, MLA_HEADS, MLA_NOPE + MLA_ROPE)
    r1 = wq[..., MLA_NOPE:MLA_NOPE + half]
    r2 = wq[..., MLA_NOPE + half:]
    nope, rot = wq[..., :MLA_NOPE], jnp.concatenate([r1, r2, r2, r1], axis=-1)
    even = jnp.concatenate([nope, rot], axis=-1)[:, 0::2]
    odd = jnp.concatenate([rot, nope], axis=-1)[:, 1::2]
    wqt = jnp.stack([even, odd], axis=2).reshape(MLA_Q_LORA, MLA_HEADS * HEAD_PAD).T.astype(BF16)

    wkv = w_kv_b.reshape(MLA_KV_LORA, MLA_HEADS, MLA_NOPE + MLA_V)
    wk = wkv[..., :MLA_NOPE].reshape(MLA_KV_LORA, MLA_HEADS * MLA_NOPE).astype(BF16)
    wvt =wkv[..., MLA_NOPE:].reshape(MLA_KV_LORA, MLA_HEADS * MLA_V).T.astype(BF16)
    return win, wqt, wk, wvt


def _rope_table(positions):
    inv = ROPE_THETA ** (-jnp.arange(0, MLA_ROPE, 2, dtype=F32) / MLA_ROPE)
    ang = inv[:, None] * positions.astype(F32).reshape(-1)[None, :]
    cos, sin = jnp.cos(ang), jnp.sin(ang)
    return jnp.concatenate([jnp.ones((MLA_NOPE, ang.shape[1]), F32), cos, cos, -sin, sin], axis=0)


def kernel(x, p, positions, norm_g, mla_w_in, mla_q_norm, mla_w_q_b, mla_kv_norm, mla_w_kv_b, mla_w_out,
           conv_w_in, conv_w, conv_w_out, mlstm_w_in, mlstm_b_gates, mlstm_w_out, ple_proj, ple_gate,
           final_norm):
    batch, seq, d = x.shape
    t = batch * seq
    xs = x.reshape(t, d)
    p_all = p.reshape(DEPTH, t, PLE_DIM)
    rope_tab = _rope_table(positions)
    fn =final_norm.reshape(1, d)
    for i in range(DEPTH):
        kind, j = i % 3, i // 3
        g = norm_g[i].reshape(1, d)
        if kind == 0:
            win, wqt, wk, wvt = _mla_weights(mla_w_in[j], mla_w_q_b[j], mla_w_kv_b[j])
            qt, k, vt, zs = _mla_proj(xs, g, win, mla_q_norm[j].reshape(1, -1), wqt,
                                      mla_kv_norm[j].reshape(1, -1), wk, wvt, rope_tab)
            a = _mla_attn(qt, k, vt, zs, batch, seq)
            w_out = mla_w_out[j]
        elif kind == 1:
            a = _conv_mix(xs, g, conv_w_in[j].astype(BF16), conv_w[j], batch, seq)
            w_out = conv_w_out[j]
        else:
            w = mlstm_w_in[j]
            n_main = w.shape[1] - 2 * MLSTM_HEADS
            wg = jnp.pad(w[:, n_main:], ((0, 0), (0, LANES - 2 * MLSTM_HEADS))).astype(BF16)
            bg = jnp.pad(mlstm_b_gates[j], (0, LANES - 2 * MLSTM_HEADS)).reshape(1, LANES)
            u, gates = _mlstm_proj(xs, g, w.astype(BF16), wg)
            a = _mlstm_chunk(u, gates, bg, batch, seq)
            w_out = mlstm_w_out[j]
        xs = _out_ple(a, w_out.astype(BF16), xs, p_all, i, ple_proj[i].astype(BF16),
                      ple_gate[i].astype(BF16), fn, final=(i == DEPTH - 1))
    return xs.reshape(batch, seq, d)
```

```python
import functools

import jax
import jax.numpy as jnp
from jax import lax
from jax.experimental import pallas as pl
from jax.experimental.pallas import tpu as pltpu

F32 = jnp.float32
BF16 = jnp.bfloat16

D_MODEL = 1024
DEPTH = 4
CHUNK = 64
PLE_DIM = 256
EPS = 1e-6
MLA_HEADS = 16
MLA_Q_LORA = 384
MLA_KV_LORA = 256
MLA_NOPE = 64
MLA_ROPE = 32
MLA_V = 64
ROPE_THETA = 10000.0
CONV_DIM = D_MODEL
MLSTM_HEADS = 4
MLSTM_INNER = 2 * D_MODEL
MLSTM_DV = MLSTM_INNER // MLSTM_HEADS
MLSTM_DK = MLSTM_DV // 2
MLSTM_U = 2 * MLSTM_HEADS * MLSTM_DK + 2 * MLSTM_INNER

LANES = 128
MXU_N = 256
HEAD_PAD = 128
NEG = -1e30
LOG2E = 1.4426950408889634
VMEM_LIMIT = 48 * 1024 * 1024

TM_OUT = 1024
TM_MLA = 512
MLA_ROWS = 256
TQ = 256
ATTN_HEADS = 4
ATTN_LAG = 5
TM_CONV = 512
CONV_ROWS = 256
TM_MLSTM = 512
MLSTM_ROWS = 256
TN_MLSTM = 512
L_MLSTM = 256
MLSTM_SEQS = 1


def _rms(x):
    return x * lax.rsqrt(jnp.mean(x * x, axis=-1, keepdims=True) + EPS)


def _dot(a, b):
    return jnp.dot(a, b, preferred_element_type=F32)


def _dot_nt(a, b):
    return lax.dot_general(a, b, (((1,), (1,)), ((), ())), preferred_element_type=F32)


def _silu(z):
    return z * jax.nn.sigmoid(z)


def _params(*sem):
    return pltpu.CompilerParams(dimension_semantics=sem, vmem_limit_bytes=VMEM_LIMIT)


def _out_ple_body(a_ref, w_ref, x_ref, p_ref, pp_ref, pg_ref, fn_ref, o_ref, *, final):
    x1 = x_ref[...] + _dot(a_ref[...], w_ref[...])
    gate = jax.nn.sigmoid(_dot(_rms(x1).astype(BF16), pg_ref[...]))
    x2 = x1 + gate * _dot(p_ref[...].astype(BF16), pp_ref[...])
    if final:
        x2 = _rms(x2) * fn_ref[...]
    o_ref[...] = x2


def _out_ple(a, w_out, x, p_all, layer, ple_proj, ple_gate, final_norm, final):
    t, k = a.shape
    tm = TM_OUT
    const = lambda i: (0, 0)
    return pl.pallas_call(
        functools.partial(_out_ple_body, final=final),
        grid=(t // tm,),
        in_specs=[
            pl.BlockSpec((tm, k), lambda i: (i, 0)),
            pl.BlockSpec((k, D_MODEL), const),
            pl.BlockSpec((tm, D_MODEL), lambda i: (i, 0)),
            pl.BlockSpec((None, tm, PLE_DIM), lambda i: (layer, i, 0)),
            pl.BlockSpec((PLE_DIM, D_MODEL), const),
            pl.BlockSpec((D_MODEL, D_MODEL), const),
            pl.BlockSpec((1, D_MODEL), const),
        ],
        out_specs=pl.BlockSpec((tm, D_MODEL), lambda i: (i, 0)),
        out_shape=jax.ShapeDtypeStruct((t, D_MODEL), F32),
        compiler_params=_params("parallel"),
        name="out_ple",
    )(a, w_out, x, p_all, ple_proj, ple_gate, final_norm)


def _mla_proj_body(x_ref, g_ref, win_ref, qn_ref, wqt_ref, kvn_ref, wk_ref, wvt_ref,
                   rope_ref, qt_ref, k_ref, vt_ref, zs_ref):
    cq_w = MLA_Q_LORA + HEAD_PAD
    z0 = MLA_KV_LORA + cq_w
    groups = [slice(r, r + MLA_ROWS) for r in range(0, x_ref.shape[0], MLA_ROWS)]
    st = [dict() for _ in groups]

    def stage_norm(n):
        rows = groups[n]
        st[n]["h"] = (_rms(x_ref[rows, :]) * g_ref[...]).astype(BF16)

    def stage_latents(n):
        rows, h = groups[n], st[n]["h"]
        ckv = _dot(h, win_ref[:, 0:MLA_KV_LORA])
        u2 = _dot(h, win_ref[:, MLA_KV_LORA:MLA_KV_LORA + cq_w])
        cq = u2[:, 0:MLA_Q_LORA]
        kr = u2[:, MLA_Q_LORA:MLA_Q_LORA + LANES]
        st[n]["cqn"] = (_rms(cq) * qn_ref[...]).astype(BF16)
        st[n]["ckvn"] = (_rms(ckv) * kvn_ref[...]).astype(BF16)
        lane = lax.broadcasted_iota(jnp.int32, kr.shape, 1)
        quarter = MLA_ROPE
        rope = rope_ref[:, rows].T
        g4 = kr * rope
        kr_hi = g4 + jnp.where(lane < MLA_NOPE + quarter, pltpu.roll(g4, LANES - quarter, 1),
                               pltpu.roll(g4, quarter, 1))
        st[n]["kr_hi"] = kr_hi
        st[n]["kr_lo"] = pltpu.roll(kr_hi, MLA_NOPE, 1)
        st[n]["low"] = lane < MLA_NOPE

    def stage_gate(n):
        rows, h = groups[n], st[n]["h"]
        for j in range(MLA_HEADS * MLA_V // MXU_N):
            sl = slice(j * MXU_N, (j + 1) * MXU_N)
            z = _dot(h, win_ref[:, z0 + j * MXU_N:z0 + (j + 1) * MXU_N])
            zs_ref[rows, sl] = _silu(z).astype(BF16)

    def stage_heads(n):
        rows, g = groups[n], st[n]
        tqt = rope_ref[:, rows] * ((MLA_NOPE + MLA_ROPE) ** -0.5 * LOG2E)
        tq2 = jnp.concatenate([tqt, tqt[MLA_NOPE:], tqt[:MLA_NOPE]], axis=0)
        for j in range(MLA_HEADS * HEAD_PAD // MXU_N):
            sl = slice(j * MXU_N, (j + 1) * MXU_N)
            qt_ref[sl, rows] = (_dot_nt(wqt_ref[sl, :], g["cqn"]) * tq2).astype(BF16)
        for c in range(MLA_HEADS * MLA_NOPE // MXU_N):
            kn = _dot(g["ckvn"], wk_ref[:, c * MXU_N:(c + 1) * MXU_N])
            for half in range(2):
                pair = kn[:, half * LANES:(half + 1) * LANES]
                col = (2 * c + half) * MXU_N
                k_ref[rows, col:col + LANES] = jnp.where(g["low"], pair, g["kr_hi"]).astype(BF16)
                k_ref[rows, col + LANES:col + MXU_N] = jnp.where(g["low"], g["kr_lo"], pair).astype(BF16)
        for j in range(MLA_HEADS * MLA_V // MXU_N):
            sl = slice(j * MXU_N, (j + 1) * MXU_N)
            vt_ref[sl, rows] = _dot_nt(wvt_ref[sl, :], g["ckvn"]).astype(BF16)
        st[n] = None

    stage_norm(0)
    stage_latents(0)
    for n in range(len(groups)):
        if n + 1 < len(groups):
            stage_norm(n + 1)
        stage_gate(n)
        if n + 1 < len(groups):
            stage_latents(n + 1)
        stage_heads(n)


def _mla_proj(x, g, win, qn, wqt, kvn, wk, wvt, rope_tab):
    t = x.shape[0]
    tm = TM_MLA
    hw = MLA_HEADS * HEAD_PAD
    vw = MLA_HEADS * MLA_V
    const = lambda i: (0, 0)
    row = lambda i: (i, 0)
    col = lambda i: (0, i)
    full = lambda arr: pl.BlockSpec(arr.shape, const)
    return pl.pallas_call(
        _mla_proj_body,
        grid=(t // tm,),
        in_specs=[pl.BlockSpec((tm, D_MODEL), row), full(g), full(win), full(qn), full(wqt), full(kvn),
                  full(wk), full(wvt),
                  pl.BlockSpec((HEAD_PAD, tm), col)],
        out_specs=[pl.BlockSpec((hw, tm), col), pl.BlockSpec((tm, hw), row),
                   pl.BlockSpec((vw, tm), col), pl.BlockSpec((tm, vw), row)],
        out_shape=[jax.ShapeDtypeStruct((hw, t), BF16), jax.ShapeDtypeStruct((t, hw), BF16),
                   jax.ShapeDtypeStruct((vw, t), BF16), jax.ShapeDtypeStruct((t, vw), BF16)],
        compiler_params=_params("parallel"),
        name="mla_proj",
    )(x, g, win, qn, wqt, kvn, wk, wvt, rope_tab)


def _mla_attn_body(qt_ref, k_ref, vt_ref, zs_ref, a_ref):
    seq = k_ref.shape[0]
    nq = seq // TQ
    key_c = lax.broadcasted_iota(jnp.int32, (TQ, TQ), 0) // CHUNK
    qry_c = lax.broadcasted_iota(jnp.int32, (TQ, TQ), 1) // CHUNK
    diag_ok = key_c <= qry_c
    units = [(i, hh, j) for i in range(nq) for hh in range(ATTN_HEADS) for j in range(i + 1)]
    lead = nq + 2
    lag = ATTN_LAG
    scores, probs = {}, {}
    m_run, l_run, acc = {}, {}, {}
    outs = {}

    def score_unit(u):
        i, hh, j = units[u]
        hs = slice(hh * HEAD_PAD, (hh + 1) * HEAD_PAD)
        s = _dot(k_ref[j * TQ:(j + 1) * TQ, hs], qt_ref[hs, i * TQ:(i + 1) * TQ])
        if j == i:
            s = jnp.where(diag_ok, s, NEG)
        mj = jnp.max(s, axis=0, keepdims=True)
        m_run[i, hh] = mj if j == 0 else jnp.maximum(m_run[i, hh], mj)
        scores[u] = s

    def exp_unit(u):
        i, hh, j = units[u]
        p = jnp.exp2(scores.pop(u) - m_run[i, hh])
        lj = jnp.sum(p, axis=0, keepdims=True)
        l_run[i, hh] = lj if j == 0 else l_run[i, hh] + lj
        probs[u] = p.astype(BF16)

    def value_unit(u):
        i, hh, j = units[u]
        oj = _dot(vt_ref[hh * MLA_V:(hh + 1) * MLA_V, j * TQ:(j + 1) * TQ], probs.pop(u))
        acc[i, hh] = oj if j == 0 else acc[i, hh] + oj

    def finish_unit(u):
        i, hh, j = units[u]
        if j != i:
            return
        outs[hh] = acc.pop((i, hh)) * (1.0 / l_run.pop((i, hh)))
        if hh % 2 == 1:
            rows = slice(i * TQ, (i + 1) * TQ)
            cols = slice((hh - 1) * MLA_V, (hh + 1) * MLA_V)
            o = jnp.concatenate([outs.pop(hh - 1), outs.pop(hh)], axis=0).T
            a_ref[rows, cols] = (o * zs_ref[rows, cols].astype(F32)).astype(BF16)

    n_units = len(units)
    for t in range(-lead, n_units + lag + 3):
        if 0 <= t + lead < n_units:
            score_unit(t + lead)
        if 0 <= t - lag < n_units:
            value_unit(t - lag)
        if 0 <= t < n_units:
            exp_unit(t)
        if 0 <= t - lag - 2 < n_units:
            finish_unit(t - lag - 2)


def _mla_attn(qt, k, vt, zs, batch, seq):
    t = k.shape[0]
    nh = ATTN_HEADS
    blk = lambda w: pl.BlockSpec((seq, w), lambda b, h: (b, h))
    blk_t = lambda w: pl.BlockSpec((w, seq), lambda b, h: (h, b))
    return pl.pallas_call(
        _mla_attn_body,
        grid=(batch, MLA_HEADS // nh),
        in_specs=[blk_t(nh * HEAD_PAD), blk(nh * HEAD_PAD), blk_t(nh * MLA_V), blk(nh * MLA_V)],
        out_specs=blk(nh * MLA_V),
        out_shape=jax.ShapeDtypeStruct((t, MLA_HEADS * MLA_V), BF16),
        compiler_params=_params("parallel", "parallel"),
        name="mla_attn",
    )(qt, k, vt, zs)


def _conv_body(x_ref, g_ref, win_ref, cw_ref, a_ref, ext_ref):
    tm = TM_CONV
    t = pl.program_id(1)

    @pl.when(t == 0)
    def _():
        ext_ref[0:8, :] = jnp.zeros((8, CONV_DIM), F32)

    @pl.when(t > 0)
    def _():
        ext_ref[0:8, :] = ext_ref[tm:tm + 8, :]

    norm = lambda r: (_rms(x_ref[r:r + CONV_ROWS, :]) * g_ref[...]).astype(BF16)
    h_next = norm(0)
    for r in range(0, tm, CONV_ROWS):
        h = h_next
        for c in range(CONV_DIM // MXU_N):
            sl = slice(c * MXU_N, (c + 1) * MXU_N)
            col = lambda part: slice(part * CONV_DIM + c * MXU_N, part * CONV_DIM + (c + 1) * MXU_N)
            prod = _dot(h, win_ref[:, col(1)]) * _dot(h, win_ref[:, col(2)])
            ext_ref[8 + r:8 + r + CONV_ROWS, sl] = prod
            y = (cw_ref[2:3, sl] * prod + cw_ref[1:2, sl] * ext_ref[7 + r:7 + r + CONV_ROWS, sl]
                 + cw_ref[0:1, sl] * ext_ref[6 + r:6 + r + CONV_ROWS, sl])
            bg = _dot(h, win_ref[:, col(0)])
            z = _dot(h, win_ref[:, col(3)])
            a_ref[r:r + CONV_ROWS, sl] = (bg * y * _silu(z)).astype(BF16)
            if c == 0 and r + CONV_ROWS < tm:
                h_next = norm(r + CONV_ROWS)


def _conv_mix(x, g, win, cw, batch, seq):
    t = x.shape[0]
    tm = TM_CONV
    nt = seq // tm
    const = lambda b, i: (0, 0)
    row = lambda b, i: (b * nt + i, 0)
    return pl.pallas_call(
        _conv_body,
        grid=(batch, nt),
        in_specs=[pl.BlockSpec((tm, D_MODEL), row), pl.BlockSpec(g.shape, const),
                  pl.BlockSpec(win.shape, const), pl.BlockSpec(cw.shape, const)],
        out_specs=pl.BlockSpec((tm, CONV_DIM), row),
        out_shape=jax.ShapeDtypeStruct((t, CONV_DIM), BF16),
        scratch_shapes=[pltpu.VMEM((tm + 8, CONV_DIM), F32)],
        compiler_params=_params("parallel", "arbitrary"),
        name="conv_mix",
    )(x, g, win, cw)


def _mlstm_proj_body(x_ref, g_ref, w_ref, wg_ref, u_ref, gates_ref):
    tn = TN_MLSTM
    qk = MLSTM_HEADS * MLSTM_DK
    cols = lambda start, j: slice(start + j * tn, start + (j + 1) * tn)
    norm = lambda r: (_rms(x_ref[r:r + MLSTM_ROWS, :]) * g_ref[...]).astype(BF16)
    h_next = norm(0)
    for r in range(0, x_ref.shape[0], MLSTM_ROWS):
        h = h_next
        rows = slice(r, r + MLSTM_ROWS)
        proj = lambda sl: _dot(h, w_ref[:, sl])
        for j in range(MLSTM_INNER // tn):
            o = proj(cols(2 * qk + MLSTM_INNER, j))
            z = proj(cols(2 * qk + 2 * MLSTM_INNER, j))
            u_ref[rows, cols(2 * qk + MLSTM_INNER, j)] = (jax.nn.sigmoid(o) * _silu(z)).astype(BF16)
            if j == 0 and r + MLSTM_ROWS < x_ref.shape[0]:
                h_next = norm(r + MLSTM_ROWS)
        gates_ref[rows, :] = _dot(h, wg_ref[...])
        for j in range(MLSTM_INNER // tn):
            u_ref[rows, cols(2 * qk, j)] = proj(cols(2 * qk, j)).astype(BF16)
        for j in range(qk // tn):
            u_ref[rows, cols(0, j)] = proj(cols(0, j)).astype(BF16)
            u_ref[rows, cols(qk, j)] = (proj(cols(qk, j)) * (MLSTM_DK ** -0.5)).astype(BF16)


def _mlstm_proj(x, g, w, wg):
    t = x.shape[0]
    n = w.shape[1] - 2 * MLSTM_HEADS
    tm = TM_MLSTM
    const = lambda i: (0, 0)
    row = lambda i: (i, 0)
    return pl.pallas_call(
        _mlstm_proj_body,
        grid=(t // tm,),
        in_specs=[pl.BlockSpec((tm, D_MODEL), row),
                  pl.BlockSpec(g.shape, const),
                  pl.BlockSpec((D_MODEL, n), const, pipeline_mode=pl.Buffered(1)),
                  pl.BlockSpec(wg.shape, const)],
        out_specs=[pl.BlockSpec((tm, MLSTM_U), row), pl.BlockSpec((tm, LANES), row)],
        out_shape=[jax.ShapeDtypeStruct((t, MLSTM_U), BF16), jax.ShapeDtypeStruct((t, LANES), F32)],
        compiler_params=_params("parallel"),
        name="mlstm_proj",
    )(x, g, w, wg)


def _mlstm_chunk_body(q_ref, k_ref, v_ref, og_ref, g_ref, bg_ref, a_ref, c_scr, m_scr):
    L = L_MLSTM
    H, DK, DV = MLSTM_HEADS, MLSTM_DK, MLSTM_DV
    NB = q_ref.shape[0]

    @pl.when(pl.program_id(1) == 0)
    def _():
        c_scr[...] = jnp.zeros_like(c_scr)
        m_scr[...] = jnp.zeros_like(m_scr)

    r_i = lax.broadcasted_iota(jnp.int32, (L, L), 0)
    c_i = lax.broadcasted_iota(jnp.int32, (L, L), 1)
    causal = c_i <= r_i
    tril = causal.astype(BF16)
    gates, b_all, b_t, g_t = [], [], [], []
    for b in range(NB):
        g = g_ref[b] + bg_ref[...]
        lf = jnp.minimum(g, 0.0) - jnp.log(1.0 + jnp.exp(-jnp.abs(g)))
        hi = lf.astype(BF16)
        r1 = lf - hi.astype(F32)
        mid = r1.astype(BF16)
        lo = (r1 - mid.astype(F32)).astype(BF16)
        cs = _dot(tril, hi) + _dot(tril, mid) + _dot(tril, lo)
        gates.append(g)
        b_all.append(cs)
        b_t.append(cs.T)
        g_t.append(g.T)

    inst = [(b, h) for b in range(NB) for h in range(H)]
    ids = range(len(inst))
    ones = jnp.ones((L, LANES), BF16)
    q = [q_ref[b, :, h * DK:(h + 1) * DK] for b, h in inst]
    k = [k_ref[b, :, h * DK:(h + 1) * DK] for b, h in inst]
    v = [jnp.concatenate([v_ref[b, :, h * DV:(h + 1) * DV], ones], axis=1) for b, h in inst]
    bc = [b_all[b][:, H + h:H + h + 1] for b, h in inst]
    igc = [gates[b][:, h:h + 1] for b, h in inst]
    m = [m_scr[i][0:1, 0:1] for i in ids]
    c_old = [c_scr[i] for i in ids]

    qk = [_dot_nt(q[i], k[i]) for i in ids]

    upd, decay, m_new = [], [], []
    for i in ids:
        b_last = bc[i][L - 1:L, :]
        d_state = b_last - bc[i] + igc[i]
        m_new.append(jnp.maximum(b_last + m[i], jnp.max(d_state, axis=0, keepdims=True)))
        w_s = jnp.exp(d_state - m_new[i])
        decay.append(jnp.exp(b_last + m[i] - m_new[i]))
        kw = k[i].astype(F32) * w_s
        upd.append(_dot(kw.T.astype(BF16), v[i]))

    d_log, gm, m_t = [], [], []
    for i, (b, h) in enumerate(inst):
        c_scr[i] = decay[i] * c_old[i] + upd[i]
        m_scr[i] = jnp.broadcast_to(m_new[i], (8, LANES))
        br = b_t[b][H + h:H + h + 1, :]
        igr = g_t[b][h:h + 1, :]
        d_log.append(jnp.where(causal, bc[i] - br + igr, NEG))
        gm.append(bc[i] + m[i])
        m_t.append(jnp.maximum(gm[i], jnp.max(d_log[i], axis=1, keepdims=True)))
    num = []
    for i in ids:
        qs = (q[i].astype(F32) * jnp.exp(gm[i] - m_t[i])).astype(BF16)
        s = (qk[i] * jnp.exp(d_log[i] - m_t[i])).astype(BF16)
        lhs = jnp.concatenate([qs, s], axis=1)
        rhs = jnp.concatenate([c_old[i].astype(BF16), v[i]], axis=0)
        num.append(_dot(lhs, rhs))
    for i, (b, h) in enumerate(inst):
        den = num[i][:, DV:DV + 1]
        hout = num[i][:, :DV] / jnp.maximum(jnp.abs(den), jnp.exp(-m_t[i]))
        sl = slice(h * DV, (h + 1) * DV)
        a_ref[b, :, sl] = (og_ref[b, :, sl].astype(F32) * hout).astype(BF16)


def _mlstm_chunk(u, gates, b_gates, batch, seq):
    t = u.shape[0]
    L, nb = L_MLSTM, MLSTM_SEQS
    qk = MLSTM_HEADS * MLSTM_DK
    u3 = u.reshape(batch, seq, u.shape[1])
    g3 = gates.reshape(batch, seq, LANES)
    blk = lambda width, col: pl.BlockSpec((nb, L, width), lambda b, c: (b, c, col))
    a = pl.pallas_call(
        _mlstm_chunk_body,
        grid=(batch // nb, seq // L),
        in_specs=[blk(qk, 0), blk(qk, 1), blk(MLSTM_INNER, 1), blk(MLSTM_INNER, 2), blk(LANES, 0),
                  pl.BlockSpec((1, LANES), lambda b, c: (0, 0))],
        out_specs=blk(MLSTM_INNER, 0),
        out_shape=jax.ShapeDtypeStruct((batch, seq, MLSTM_INNER), BF16),
        scratch_shapes=[pltpu.VMEM((nb * MLSTM_HEADS, MLSTM_DK, MLSTM_DV + LANES), F32),
                        pltpu.VMEM((nb * MLSTM_HEADS, 8, LANES), F32)],
        compiler_params=_params("parallel", "arbitrary"),
        name="mlstm_chunk",
    )(u3, u3, u3, u3, g3, b_gates)
    return a.reshape(t, MLSTM_INNER)


def _mla_weights(w_in, w_q_b, w_kv_b):
    half = MLA_ROPE // 2
    c_q = w_in[:, :MLA_Q_LORA]
    c_kv = w_in[:, MLA_Q_LORA:MLA_Q_LORA + MLA_KV_LORA]
    r0 = MLA_Q_LORA + MLA_KV_LORA
    k1 = w_in[:, r0:r0 + half]
    k2 = w_in[:, r0 + half:r0 + MLA_ROPE]
    z = w_in[:, r0 + MLA_ROPE:]
    pad = jnp.zeros((D_MODEL, LANES - 4 * half), w_in.dtype)
    win = jnp.concatenate([c_kv, c_q, pad, k1, k2, k2, k1, z], axis=1).astype(BF16)

    wq = w_q_b.reshape(MLA_Q_LORA, MLA_HEADS, MLA_NOPE + MLA_ROPE)
    r1 = wq[..., MLA_NOPE:MLA_NOPE + half]
    r2 = wq[..., MLA_NOPE + half:]
    nope, rot = wq[..., :MLA_NOPE], jnp.concatenate([r1, r2, r2, r1], axis=-1)
    even = jnp.concatenate([nope, rot], axis=-1)[:, 0::2]
    odd = jnp.concatenate([rot, nope], axis=-1)[:, 1::2]
    wqt = jnp.stack([even, odd], axis=2).reshape(MLA_Q_LORA, MLA_HEADS * HEAD_PAD).T.astype(BF16)

    wkv = w_kv_b.reshape(MLA_KV_LORA, MLA_HEADS, MLA_NOPE + MLA_V)
    wk = wkv[..., :MLA_NOPE].reshape(MLA_KV_LORA, MLA_HEADS * MLA_NOPE).astype(BF16)
    wvt = wkv[..., MLA_NOPE:].reshape(MLA_KV_LORA, MLA_HEADS * MLA_V).T.astype(BF16)
    return win, wqt, wk, wvt


def _rope_table(positions):
    inv = ROPE_THETA ** (-jnp.arange(0, MLA_ROPE, 2, dtype=F32) / MLA_ROPE)
    ang = inv[:, None] * positions.astype(F32).reshape(-1)[None, :]
    cos, sin = jnp.cos(ang), jnp.sin(ang)
    return jnp.concatenate([jnp.ones((MLA_NOPE, ang.shape[1]), F32), cos, cos, -sin, sin], axis=0)


def kernel(x, p, positions, norm_g, mla_w_in, mla_q_norm, mla_w_q_b, mla_kv_norm, mla_w_kv_b, mla_w_out,
           conv_w_in, conv_w, conv_w_out, mlstm_w_in, mlstm_b_gates, mlstm_w_out, ple_proj, ple_gate,
           final_norm):
    batch, seq, d = x.shape
    t = batch * seq
    assert d == D_MODEL and norm_g.shape[0] == DEPTH
    assert seq % TQ == 0 and seq % TM_CONV == 0 and seq % L_MLSTM == 0 and batch % MLSTM_SEQS == 0
    assert t % TM_OUT == 0 and t % TM_MLA == 0 and t % TM_MLSTM == 0
    xs = x.reshape(t, d)
    p_all = p.reshape(DEPTH, t, PLE_DIM)
    rope_tab = _rope_table(positions)
    fn = final_norm.reshape(1, d)
    for i in range(DEPTH):
        kind, j = i % 3, i // 3
        g = norm_g[i].reshape(1, d)
        if kind == 0:
            win, wqt, wk, wvt = _mla_weights(mla_w_in[j], mla_w_q_b[j], mla_w_kv_b[j])
            qt, k, vt, zs = _mla_proj(xs, g, win, mla_q_norm[j].reshape(1, -1), wqt,
                                      mla_kv_norm[j].reshape(1, -1), wk, wvt, rope_tab)
            a = _mla_attn(qt, k, vt, zs, batch, seq)
            w_out = mla_w_out[j]
        elif kind == 1:
            a = _conv_mix(xs, g, conv_w_in[j].astype(BF16), conv_w[j], batch, seq)
            w_out = conv_w_out[j]
        else:
            w = mlstm_w_in[j]
            n_main = w.shape[1] - 2 * MLSTM_HEADS
            wg = jnp.pad(w[:, n_main:], ((0, 0), (0, LANES - 2 * MLSTM_HEADS))).astype(BF16)
            bg = jnp.pad(mlstm_b_gates[j], (0, LANES - 2 * MLSTM_HEADS)).reshape(1, LANES)
            u, gates = _mlstm_proj(xs, g, w.astype(BF16), wg)
            a = _mlstm_chunk(u, gates, bg, batch, seq)
            w_out = mlstm_w_out[j]
        xs = _out_ple(a, w_out.astype(BF16), xs, p_all, i, ple_proj[i].astype(BF16),
                      ple_gate[i].astype(BF16), fn, final=(i == DEPTH - 1))
    return xs.reshape(batch, seq, d)
```

```python
import functools

import jax
import jax.numpy as jnp
from jax import lax
from jax.experimental import pallas as pl
from jax.experimental.pallas import tpu as pltpu

F32 = jnp.float32
BF16 = jnp.bfloat16

D_MODEL = 1024
DEPTH = 4
CHUNK = 64
PLE_DIM = 256
EPS = 1e-6
MLA_HEADS = 16
MLA_Q_LORA = 384
MLA_KV_LORA = 256
MLA_NOPE = 64
MLA_ROPE = 32
MLA_V = 64
ROPE_THETA = 10000.0
CONV_DIM = D_MODEL
MLSTM_HEADS = 4
MLSTM_INNER = 2 * D_MODEL
MLSTM_DV = MLSTM_INNER // MLSTM_HEADS
MLSTM_DK = MLSTM_DV // 2
MLSTM_U = 2 * MLSTM_HEADS * MLSTM_DK + 2 * MLSTM_INNER

LANES = 128
MXU_N = 256
HEAD_PAD = 128
NEG = -1e30
LOG2E = 1.4426950408889634
VMEM_LIMIT = 48 * 1024 * 1024

TM_OUT = 1024
TM_MLA = 512
MLA_ROWS = 256
TQ = 256
ATTN_HEADS = 4
ATTN_LAG = 5
TM_CONV = 512
CONV_ROWS = 256
TM_MLSTM = 512
MLSTM_ROWS = 256
TN_MLSTM = 512
L_MLSTM = 256
MLSTM_SEQS = 1


def _rms(x):
    return x * lax.rsqrt(jnp.mean(x * x, axis=-1, keepdims=True) + EPS)


def _dot(a, b):
    return jnp.dot(a, b, preferred_element_type=F32)


def _dot_nt(a, b):
    return lax.dot_general(a, b, (((1,), (1,)), ((), ())), preferred_element_type=F32)


def _silu(z):
    return z * jax.nn.sigmoid(z)


def _params(*sem):
    return pltpu.CompilerParams(dimension_semantics=sem, vmem_limit_bytes=VMEM_LIMIT)


def _out_ple_body(a_ref, w_ref, x_ref, p_ref, pp_ref, pg_ref, fn_ref, o_ref, *, final):
    x1 = x_ref[...] + _dot(a_ref[...], w_ref[...])
    gate = jax.nn.sigmoid(_dot(_rms(x1).astype(BF16), pg_ref[...]))
    x2 = x1 + gate * _dot(p_ref[...].astype(BF16), pp_ref[...])
    if final:
        x2 = _rms(x2) * fn_ref[...]
    o_ref[...] = x2


def _out_ple(a, w_out, x, p_all, layer, ple_proj, ple_gate, final_norm, final):
    t, k = a.shape
    tm = TM_OUT
    const = lambda i: (0, 0)
    return pl.pallas_call(
        functools.partial(_out_ple_body, final=final),
        grid=(t // tm,),
        in_specs=[
            pl.BlockSpec((tm, k), lambda i: (i, 0)),
            pl.BlockSpec((k, D_MODEL), const),
            pl.BlockSpec((tm, D_MODEL), lambda i: (i, 0)),
            pl.BlockSpec((None, tm, PLE_DIM), lambda i: (layer, i, 0)),
            pl.BlockSpec((PLE_DIM, D_MODEL), const),
            pl.BlockSpec((D_MODEL, D_MODEL), const),
            pl.BlockSpec((1, D_MODEL), const),
        ],
        out_specs=pl.BlockSpec((tm, D_MODEL), lambda i: (i, 0)),
        out_shape=jax.ShapeDtypeStruct((t, D_MODEL), F32),
        compiler_params=_params("parallel"),
        name="out_ple",
    )(a, w_out, x, p_all, ple_proj, ple_gate, final_norm)


def _mla_proj_body(x_ref, g_ref, win_ref, qn_ref, wqt_ref, kvn_ref, wk_ref, wvt_ref,
                   rope_ref, qt_ref, k_ref, vt_ref, zs_ref):
    cq_w = MLA_Q_LORA + HEAD_PAD
    z0 = MLA_KV_LORA + cq_w
    groups = [slice(r, r + MLA_ROWS) for r in range(0, x_ref.shape[0], MLA_ROWS)]
    st = [dict() for _ in groups]

    def stage_norm(n):
        rows = groups[n]
        st[n]["h"] = (_rms(x_ref[rows, :]) * g_ref[...]).astype(BF16)

    def stage_latents(n):
        rows, h = groups[n], st[n]["h"]
        ckv = _dot(h, win_ref[:, 0:MLA_KV_LORA])
        u2 = _dot(h, win_ref[:, MLA_KV_LORA:MLA_KV_LORA + cq_w])
        cq = u2[:, 0:MLA_Q_LORA]
        kr = u2[:, MLA_Q_LORA:MLA_Q_LORA + LANES]
        st[n]["cqn"] = (_rms(cq) * qn_ref[...]).astype(BF16)
        st[n]["ckvn"] = (_rms(ckv) * kvn_ref[...]).astype(BF16)
        lane = lax.broadcasted_iota(jnp.int32, kr.shape, 1)
        quarter = MLA_ROPE
        rope = rope_ref[:, rows].T
        g4 = kr * rope
        kr_hi = g4 + jnp.where(lane < MLA_NOPE + quarter, pltpu.roll(g4, LANES - quarter, 1),
                               pltpu.roll(g4, quarter, 1))
        st[n]["kr_hi"] = kr_hi
        st[n]["kr_lo"] = pltpu.roll(kr_hi, MLA_NOPE, 1)
        st[n]["low"] = lane < MLA_NOPE

    def stage_gate(n):
        rows, h = groups[n], st[n]["h"]
        for j in range(MLA_HEADS * MLA_V // MXU_N):
            sl = slice(j * MXU_N, (j + 1) * MXU_N)
            z = _dot(h, win_ref[:, z0 + j * MXU_N:z0 + (j + 1) * MXU_N])
            zs_ref[rows, sl] = _silu(z).astype(BF16)

    def stage_heads(n):
        rows, g = groups[n], st[n]
        tqt = rope_ref[:, rows] * ((MLA_NOPE + MLA_ROPE) ** -0.5 * LOG2E)
        tq2 = jnp.concatenate([tqt, tqt[MLA_NOPE:], tqt[:MLA_NOPE]], axis=0)
        for j in range(MLA_HEADS * HEAD_PAD // MXU_N):
            sl = slice(j * MXU_N, (j + 1) * MXU_N)
            qt_ref[sl, rows] = (_dot_nt(wqt_ref[sl, :], g["cqn"]) * tq2).astype(BF16)
        for c in range(MLA_HEADS * MLA_NOPE // MXU_N):
            kn = _dot(g["ckvn"], wk_ref[:, c * MXU_N:(c + 1) * MXU_N])
            for half in range(2):
                pair = kn[:, half * LANES:(half + 1) * LANES]
                col = (2 * c + half) * MXU_N
                k_ref[rows, col:col + LANES] = jnp.where(g["low"], pair, g["kr_hi"]).astype(BF16)
                k_ref[rows, col + LANES:col + MXU_N] = jnp.where(g["low"], g["kr_lo"], pair).astype(BF16)
        for j in range(MLA_HEADS * MLA_V // MXU_N):
            sl = slice(j * MXU_N, (j + 1) * MXU_N)
            vt_ref[sl, rows] = _dot_nt(wvt_ref[sl, :], g["ckvn"]).astype(BF16)
        st[n] = None

    stage_norm(0)
    stage_latents(0)
    for n in range(len(groups)):
        if n + 1 < len(groups):
            stage_norm(n + 1)
        stage_gate(n)
        if n + 1 < len(groups):
            stage_latents(n + 1)
        stage_heads(n)


def _mla_proj(x, g, win, qn, wqt, kvn, wk, wvt, rope_tab):
    t = x.shape[0]
    tm = TM_MLA
    hw = MLA_HEADS * HEAD_PAD
    vw = MLA_HEADS * MLA_V
    const = lambda i: (0, 0)
    row = lambda i: (i, 0)
    col = lambda i: (0, i)
    full = lambda arr: pl.BlockSpec(arr.shape, const)
    return pl.pallas_call(
        _mla_proj_body,
        grid=(t // tm,),
        in_specs=[pl.BlockSpec((tm, D_MODEL), row), full(g), full(win), full(qn), full(wqt), full(kvn),
                  full(wk), full(wvt),
                  pl.BlockSpec((HEAD_PAD, tm), col)],
        out_specs=[pl.BlockSpec((hw, tm), col), pl.BlockSpec((tm, hw), row),
                   pl.BlockSpec((vw, tm), col), pl.BlockSpec((tm, vw), row)],
        out_shape=[jax.ShapeDtypeStruct((hw, t), BF16), jax.ShapeDtypeStruct((t, hw), BF16),
                   jax.ShapeDtypeStruct((vw, t), BF16), jax.ShapeDtypeStruct((t, vw), BF16)],
        compiler_params=_params("parallel"),
        name="mla_proj",
    )(x, g, win, qn, wqt, kvn, wk, wvt, rope_tab)


def _mla_attn_body(qt_ref, k_ref, vt_ref, zs_ref, a_ref):
    seq = k_ref.shape[0]
    nq = seq // TQ
    key_c = lax.broadcasted_iota(jnp.int32, (TQ, TQ), 0) // CHUNK
    qry_c = lax.broadcasted_iota(jnp.int32, (TQ, TQ), 1) // CHUNK
    diag_ok = key_c <= qry_c
    units = [(i, hh, j) for i in range(nq) for hh in range(ATTN_HEADS) for j in range(i + 1)]
    lead = nq + 2
    lag = ATTN_LAG
    scores, probs = {}, {}
    m_run, l_run, acc = {}, {}, {}
    outs = {}

    def score_unit(u):
        i, hh, j = units[u]
        hs = slice(hh * HEAD_PAD, (hh + 1) * HEAD_PAD)
        s = _dot(k_ref[j * TQ:(j + 1) * TQ, hs], qt_ref[hs, i * TQ:(i + 1) * TQ])
        if j == i:
            s = jnp.where(diag_ok, s, NEG)
        mj = jnp.max(s.reshape(TQ // 8, 8, TQ), axis=0)
        m8 = mj if j == 0 else jnp.maximum(m_run[i, hh], mj)
        m_run[i, hh] = jnp.max(m8, axis=0, keepdims=True) if j == i else m8
        scores[u] = s

    def exp_unit(u):
        i, hh, j = units[u]
        p = jnp.exp2(scores.pop(u) - m_run[i, hh])
        lj = jnp.sum(p.reshape(TQ // 8, 8, TQ), axis=0)
        l8 = lj if j == 0 else l_run[i, hh] + lj
        l_run[i, hh] = jnp.sum(l8, axis=0, keepdims=True) if j == i else l8
        probs[u] = p.astype(BF16)

    def value_unit(u):
        i, hh, j = units[u]
        oj = _dot(vt_ref[hh * MLA_V:(hh + 1) * MLA_V, j * TQ:(j + 1) * TQ], probs.pop(u))
        acc[i, hh] = oj if j == 0 else acc[i, hh] + oj

    def finish_unit(u):
        i, hh, j = units[u]
        if j != i:
            return
        outs[hh] = acc.pop((i, hh)) * (1.0 / l_run.pop((i, hh)))
        if hh % 2 == 1:
            rows = slice(i * TQ, (i + 1) * TQ)
            cols = slice((hh - 1) * MLA_V, (hh + 1) * MLA_V)
            o = jnp.concatenate([outs.pop(hh - 1), outs.pop(hh)], axis=0).T
            a_ref[rows, cols] = (o * zs_ref[rows, cols].astype(F32)).astype(BF16)

    n_units = len(units)
    for t in range(-lead, n_units + lag + 3):
        if 0 <= t + lead < n_units:
            score_unit(t + lead)
        if 0 <= t - lag < n_units:
            value_unit(t - lag)
        if 0 <= t < n_units:
            exp_unit(t)
        if 0 <= t - lag - 2 < n_units:
            finish_unit(t - lag - 2)


def _mla_attn(qt, k, vt, zs, batch, seq):
    t = k.shape[0]
    nh = ATTN_HEADS
    blk = lambda w: pl.BlockSpec((seq, w), lambda b, h: (b, h))
    blk_t = lambda w: pl.BlockSpec((w, seq), lambda b, h: (h, b))
    return pl.pallas_call(
        _mla_attn_body,
        grid=(batch, MLA_HEADS // nh),
        in_specs=[blk_t(nh * HEAD_PAD), blk(nh * HEAD_PAD), blk_t(nh * MLA_V), blk(nh * MLA_V)],
        out_specs=blk(nh * MLA_V),
        out_shape=jax.ShapeDtypeStruct((t, MLA_HEADS * MLA_V), BF16),
        compiler_params=_params("parallel", "parallel"),
        name="mla_attn",
    )(qt, k, vt, zs)


def _conv_body(x_ref, g_ref, win_ref, cw_ref, a_ref, ext_ref):
    tm = TM_CONV
    t = pl.program_id(1)

    @pl.when(t == 0)
    def _():
        ext_ref[0:8, :] = jnp.zeros((8, CONV_DIM), F32)

    @pl.when(t > 0)
    def _():
        ext_ref[0:8, :] = ext_ref[tm:tm + 8, :]

    norm = lambda r: (_rms(x_ref[r:r + CONV_ROWS, :]) * g_ref[...]).astype(BF16)
    h_next = norm(0)
    for r in range(0, tm, CONV_ROWS):
        h = h_next
        for c in range(CONV_DIM // MXU_N):
            sl = slice(c * MXU_N, (c + 1) * MXU_N)
            col = lambda part: slice(part * CONV_DIM + c * MXU_N, part * CONV_DIM + (c + 1) * MXU_N)
            prod = _dot(h, win_ref[:, col(1)]) * _dot(h, win_ref[:, col(2)])
            ext_ref[8 + r:8 + r + CONV_ROWS, sl] = prod
            y = (cw_ref[2:3, sl] * prod + cw_ref[1:2, sl] * ext_ref[7 + r:7 + r + CONV_ROWS, sl]
                 + cw_ref[0:1, sl] * ext_ref[6 + r:6 + r + CONV_ROWS, sl])
            bg = _dot(h, win_ref[:, col(0)])
            z = _dot(h, win_ref[:, col(3)])
            a_ref[r:r + CONV_ROWS, sl] = (bg * y * _silu(z)).astype(BF16)
            if c == 0 and r + CONV_ROWS < tm:
                h_next = norm(r + CONV_ROWS)


def _conv_mix(x, g, win, cw, batch, seq):
    t = x.shape[0]
    tm = TM_CONV
    nt = seq // tm
    const = lambda b, i: (0, 0)
    row = lambda b, i: (b * nt + i, 0)
    return pl.pallas_call(
        _conv_body,
        grid=(batch, nt),
        in_specs=[pl.BlockSpec((tm, D_MODEL), row), pl.BlockSpec(g.shape, const),
                  pl.BlockSpec(win.shape, const), pl.BlockSpec(cw.shape, const)],
        out_specs=pl.BlockSpec((tm, CONV_DIM), row),
        out_shape=jax.ShapeDtypeStruct((t, CONV_DIM), BF16),
        scratch_shapes=[pltpu.VMEM((tm + 8, CONV_DIM), F32)],
        compiler_params=_params("parallel", "arbitrary"),
        name="conv_mix",
    )(x, g, win, cw)


def _mlstm_proj_body(x_ref, g_ref, w_ref, wg_ref, u_ref, gates_ref):
    tn = TN_MLSTM
    qk = MLSTM_HEADS * MLSTM_DK
    cols = lambda start, j: slice(start + j * tn, start + (j + 1) * tn)
    norm = lambda r: (_rms(x_ref[r:r + MLSTM_ROWS, :]) * g_ref[...]).astype(BF16)
    h_next = norm(0)
    for r in range(0, x_ref.shape[0], MLSTM_ROWS):
        h = h_next
        rows = slice(r, r + MLSTM_ROWS)
        proj = lambda sl: _dot(h, w_ref[:, sl])
        for j in range(MLSTM_INNER // tn):
            o = proj(cols(2 * qk + MLSTM_INNER, j))
            z = proj(cols(2 * qk + 2 * MLSTM_INNER, j))
            u_ref[rows, cols(2 * qk + MLSTM_INNER, j)] = (jax.nn.sigmoid(o) * _silu(z)).astype(BF16)
            if j == 0 and r + MLSTM_ROWS < x_ref.shape[0]:
                h_next = norm(r + MLSTM_ROWS)
        gates_ref[rows, :] = _dot(h, wg_ref[...])
        for j in range(MLSTM_INNER // tn):
            u_ref[rows, cols(2 * qk, j)] = proj(cols(2 * qk, j)).astype(BF16)
        for j in range(qk // tn):
            u_ref[rows, cols(0, j)] = proj(cols(0, j)).astype(BF16)
            u_ref[rows, cols(qk, j)] = (proj(cols(qk, j)) * (MLSTM_DK ** -0.5)).astype(BF16)


def _mlstm_proj(x, g, w, wg):
    t = x.shape[0]
    n = w.shape[1] - 2 * MLSTM_HEADS
    tm = TM_MLSTM
    const = lambda i: (0, 0)
    row = lambda i: (i, 0)
    return pl.pallas_call(
        _mlstm_proj_body,
        grid=(t // tm,),
        in_specs=[pl.BlockSpec((tm, D_MODEL), row),
                  pl.BlockSpec(g.shape, const),
                  pl.BlockSpec((D_MODEL, n), const, pipeline_mode=pl.Buffered(1)),
                  pl.BlockSpec(wg.shape, const)],
        out_specs=[pl.BlockSpec((tm, MLSTM_U), row), pl.BlockSpec((tm, LANES), row)],
        out_shape=[jax.ShapeDtypeStruct((t, MLSTM_U), BF16), jax.ShapeDtypeStruct((t, LANES), F32)],
        compiler_params=_params("parallel"),
        name="mlstm_proj",
    )(x, g, w, wg)


def _mlstm_chunk_body(q_ref, k_ref, v_ref, og_ref, g_ref, bg_ref, a_ref, c_scr, m_scr):
    L = L_MLSTM
    H, DK, DV = MLSTM_HEADS, MLSTM_DK, MLSTM_DV
    NB = q_ref.shape[0]

    @pl.when(pl.program_id(1) == 0)
    def _():
        c_scr[...] = jnp.zeros_like(c_scr)
        m_scr[...] = jnp.zeros_like(m_scr)

    r_i = lax.broadcasted_iota(jnp.int32, (L, L), 0)
    c_i = lax.broadcasted_iota(jnp.int32, (L, L), 1)
    causal = c_i <= r_i
    tril = causal.astype(BF16)
    gates, b_all, b_t, g_t = [], [], [], []
    for b in range(NB):
        g = g_ref[b] + bg_ref[...]
        lf = jnp.minimum(g, 0.0) - jnp.log(1.0 + jnp.exp(-jnp.abs(g)))
        hi = lf.astype(BF16)
        r1 = lf - hi.astype(F32)
        mid = r1.astype(BF16)
        lo = (r1 - mid.astype(F32)).astype(BF16)
        cs = _dot(tril, hi) + _dot(tril, mid) + _dot(tril, lo)
        gates.append(g)
        b_all.append(cs)
        b_t.append(cs.T)
        g_t.append(g.T)

    inst = [(b, h) for b in range(NB) for h in range(H)]
    ids = range(len(inst))
    ones = jnp.ones((L, LANES), BF16)
    q = [q_ref[b, :, h * DK:(h + 1) * DK] for b, h in inst]
    k = [k_ref[b, :, h * DK:(h + 1) * DK] for b, h in inst]
    v = [jnp.concatenate([v_ref[b, :, h * DV:(h + 1) * DV], ones], axis=1) for b, h in inst]
    bc = [b_all[b][:, H + h:H + h + 1] for b, h in inst]
    igc = [gates[b][:, h:h + 1] for b, h in inst]
    m = [m_scr[i][0:1, 0:1] for i in ids]
    c_old = [c_scr[i] for i in ids]

    qk = [_dot_nt(q[i], k[i]) for i in ids]

    upd, decay, m_new = [], [], []
    for i in ids:
        b_last = bc[i][L - 1:L, :]
        d_state = b_last - bc[i] + igc[i]
        m_new.append(jnp.maximum(b_last + m[i], jnp.max(d_state, axis=0, keepdims=True)))
        w_s = jnp.exp(d_state - m_new[i])
        decay.append(jnp.exp(b_last + m[i] - m_new[i]))
        kw = k[i].astype(F32) * w_s
        upd.append(_dot(kw.T.astype(BF16), v[i]))

    d_log, gm, m_t = [], [], []
    for i, (b, h) in enumerate(inst):
        c_scr[i] = decay[i] * c_old[i] + upd[i]
        m_scr[i] = jnp.broadcast_to(m_new[i], (8, LANES))
        br = b_t[b][H + h:H + h + 1, :]
        igr = g_t[b][h:h + 1, :]
        d_log.append(jnp.where(causal, bc[i] - br + igr, NEG))
        gm.append(bc[i] + m[i])
        m_t.append(jnp.maximum(gm[i], jnp.max(d_log[i], axis=1, keepdims=True)))
    num = []
    for i in ids:
        qs = (q[i].astype(F32) * jnp.exp(gm[i] - m_t[i])).astype(BF16)
        s = (qk[i] * jnp.exp(d_log[i] - m_t[i])).astype(BF16)
        lhs = jnp.concatenate([qs, s], axis=1)
        rhs = jnp.concatenate([c_old[i].astype(BF16), v[i]], axis=0)
        num.append(_dot(lhs, rhs))
    for i, (b, h) in enumerate(inst):
        den = num[i][:, DV:DV + 1]
        hout = num[i][:, :DV] / jnp.maximum(jnp.abs(den), jnp.exp(-m_t[i]))
        sl = slice(h * DV, (h + 1) * DV)
        a_ref[b, :, sl] = (og_ref[b, :, sl].astype(F32) * hout).astype(BF16)


def _mlstm_chunk(u, gates, b_gates, batch, seq):
    t = u.shape[0]
    L, nb = L_MLSTM, MLSTM_SEQS
    qk = MLSTM_HEADS * MLSTM_DK
    u3 = u.reshape(batch, seq, u.shape[1])
    g3 = gates.reshape(batch, seq, LANES)
    blk = lambda width, col: pl.BlockSpec((nb, L, width), lambda b, c: (b, c, col))
    a = pl.pallas_call(
        _mlstm_chunk_body,
        grid=(batch // nb, seq // L),
        in_specs=[blk(qk, 0), blk(qk, 1), blk(MLSTM_INNER, 1), blk(MLSTM_INNER, 2), blk(LANES, 0),
                  pl.BlockSpec((1, LANES), lambda b, c: (0, 0))],
        out_specs=blk(MLSTM_INNER, 0),
        out_shape=jax.ShapeDtypeStruct((batch, seq, MLSTM_INNER), BF16),
        scratch_shapes=[pltpu.VMEM((nb * MLSTM_HEADS, MLSTM_DK, MLSTM_DV + LANES), F32),
                        pltpu.VMEM((nb * MLSTM_HEADS, 8, LANES), F32)],
        compiler_params=_params("parallel", "arbitrary"),
        name="mlstm_chunk",
    )(u3, u3, u3, u3, g3, b_gates)
    return a.reshape(t, MLSTM_INNER)


def _mla_weights(w_in, w_q_b, w_kv_b):
    half = MLA_ROPE // 2
    c_q = w_in[:, :MLA_Q_LORA]
    c_kv = w_in[:, MLA_Q_LORA:MLA_Q_LORA + MLA_KV_LORA]
    r0 = MLA_Q_LORA + MLA_KV_LORA
    k1 = w_in[:, r0:r0 + half]
    k2 = w_in[:, r0 + half:r0 + MLA_ROPE]
    z = w_in[:, r0 + MLA_ROPE:]
    pad = jnp.zeros((D_MODEL, LANES - 4 * half), w_in.dtype)
    win = jnp.concatenate([c_kv, c_q, pad, k1, k2, k2, k1, z], axis=1).astype(BF16)

    wq = w_q_b.reshape(MLA_Q_LORA, MLA_HEADS, MLA_NOPE + MLA_ROPE)
    r1 = wq[..., MLA_NOPE:MLA_NOPE + half]
    r2 = wq[..., MLA_NOPE + half:]
    nope, rot = wq[..., :MLA_NOPE], jnp.concatenate([r1, r2, r2, r1], axis=-1)
    even = jnp.concatenate([nope, rot], axis=-1)[:, 0::2]
    odd = jnp.concatenate([rot, nope], axis=-1)[:, 1::2]
    wqt = jnp.stack([even, odd], axis=2).reshape(MLA_Q_LORA, MLA_HEADS * HEAD_PAD).T.astype(BF16)

    wkv = w_kv_b.reshape(MLA_KV_LORA, MLA_HEADS, MLA_NOPE + MLA_V)
    wk = wkv[..., :MLA_NOPE].reshape(MLA_KV_LORA, MLA_HEADS * MLA_NOPE).astype(BF16)
    wvt = wkv[..., MLA_NOPE:].reshape(MLA_KV_LORA, MLA_HEADS * MLA_V).T.astype(BF16)
    return win, wqt, wk, wvt


def _rope_table(positions):
    inv = ROPE_THETA ** (-jnp.arange(0, MLA_ROPE, 2, dtype=F32) / MLA_ROPE)
    ang = inv[:, None] * positions.astype(F32).reshape(-1)[None, :]
    cos, sin = jnp.cos(ang), jnp.sin(ang)
    return jnp.concatenate([jnp.ones((MLA_NOPE, ang.shape[1]), F32), cos, cos, -sin, sin], axis=0)


def kernel(x, p, positions, norm_g, mla_w_in, mla_q_norm, mla_w_q_b, mla_kv_norm, mla_w_kv_b, mla_w_out,
           conv_w_in, conv_w, conv_w_out, mlstm_w_in, mlstm_b_gates, mlstm_w_out, ple_proj, ple_gate,
           final_norm):
    batch, seq, d = x.shape
    t = batch * seq
    assert d == D_MODEL and norm_g.shape[0] == DEPTH
    assert seq % TQ == 0 and seq % TM_CONV == 0 and seq % L_MLSTM == 0 and batch % MLSTM_SEQS == 0
    assert t % TM_OUT == 0 and t % TM_MLA == 0 and t % TM_MLSTM == 0
    xs = x.reshape(t, d)
    p_all = p.reshape(DEPTH, t, PLE_DIM)
    rope_tab = _rope_table(positions)
    fn = final_norm.reshape(1, d)
    for i in range(DEPTH):
        kind, j = i % 3, i // 3
        g = norm_g[i].reshape(1, d)
        if kind == 0:
            win, wqt, wk, wvt = _mla_weights(mla_w_in[j], mla_w_q_b[j], mla_w_kv_b[j])
            qt, k, vt, zs = _mla_proj(xs, g, win, mla_q_norm[j].reshape(1, -1), wqt,
                                      mla_kv_norm[j].reshape(1, -1), wk, wvt, rope_tab)
            a = _mla_attn(qt, k, vt, zs, batch, seq)
            w_out = mla_w_out[j]
        elif kind == 1:
            a = _conv_mix(xs, g, conv_w_in[j].astype(BF16), conv_w[j], batch, seq)
            w_out = conv_w_out[j]
        else:
            w = mlstm_w_in[j]
            n_main = w.shape[1] - 2 * MLSTM_HEADS
            wg = jnp.pad(w[:, n_main:], ((0, 0), (0, LANES - 2 * MLSTM_HEADS))).astype(BF16)
            bg = jnp.pad(mlstm_b_gates[j], (0, LANES - 2 * MLSTM_HEADS)).reshape(1, LANES)
            u, gates = _mlstm_proj(xs, g, w.astype(BF16), wg)
            a = _mlstm_chunk(u, gates, bg, batch, seq)
            w_out = mlstm_w_out[j]
        xs = _out_ple(a, w_out.astype(BF16), xs, p_all, i, ple_proj[i].astype(BF16),
                      ple_gate[i].astype(BF16), fn, final=(i == DEPTH - 1))
    return xs.reshape(batch, seq, d)
```
